```python
import jax, jax.numpy as jnp
from jax import lax
import numpy as np

D_MODEL = 2048
BATCH = 1
SEQ = 8192
DEPTH = 1

CTX_LEN = 256
GRID_W = 64
D_MIX = D_MODEL
W_CONV = D_MIX // 2
W_LRU = D_MIX - W_CONV
N_CONV_GROUPS = 16
N_LRU_HEADS = 16
LRU_HEAD_DIM = W_LRU // N_LRU_HEADS
CONV_A_WIDTH = 3
CONV_A_LEFT = 1
CONV_B_WIDTH = 4
CONV_B_LEFT = 2
LRU_C = 8.0
N_DIR = 2
D_IN_PROJ = 4 * W_CONV + 2 * W_LRU
EPS = 1e-6

kernel_name = "hybrid_conv_rglru_parallel_heads_dit"


def _rmsnorm(x, g):
    xf = x.astype(jnp.float32)
    y = xf * lax.rsqrt(jnp.mean(xf * xf, axis=-1, keepdims=True) + EPS)
    return (y * g.astype(jnp.float32)).astype(x.dtype)


def _dwconv(x, w, axis, left):
    k = w.shape[0]
    n = x.shape[axis]
    pad = [(0, 0)] * x.ndim
    pad[axis] = (left, k - 1 - left)
    xp = jnp.pad(x, pad)
    out = lax.slice_in_dim(xp, 0, n, axis=axis) * w[0]
    for j in range(1, k):
        out = out + lax.slice_in_dim(xp, j, j + n, axis=axis) * w[j]
    return out


def _conv_latent(x, w, left):
    b, l, ch = x.shape
    rows = l // GRID_W
    return _dwconv(x.reshape(b, rows, GRID_W, ch), w, 2, left).reshape(b, l, ch)


def _conv_context(x, w, left):
    return _dwconv(x, w, 1, left)


def _blockdiag(x, w, b):
    bsz, l, _ = x.shape
    y = jnp.einsum('blhi,hij->blhj', x.reshape(bsz, l, N_LRU_HEADS, LRU_HEAD_DIM), w)
    return y.reshape(bsz, l, W_LRU) + b


def _lru_coeffs(xb, wa, ba, wx, bx, lam):
    xf = xb.astype(jnp.float32)
    r = jax.nn.sigmoid(_blockdiag(xf, wa.astype(jnp.float32), ba.astype(jnp.float32)))
    i = jax.nn.sigmoid(_blockdiag(xf, wx.astype(jnp.float32), bx.astype(jnp.float32)))
    log_a = -LRU_C * r * jax.nn.softplus(-lam.astype(jnp.float32))
    a = jnp.exp(log_a)
    bterm = jnp.sqrt(-jnp.expm1(2.0 * log_a)) * (i * xf)
    return a, bterm


def _combine(e1, e2):
    a1, b1 = e1
    a2, b2 = e2
    return a1 * a2, a2 * b1 + b2


def _linear_scan(a, b, h0, reverse):
    if h0 is not None:
        idx = -1 if reverse else 0
        b = b.at[:, idx].add(a[:, idx] * h0)
    _, h = lax.associative_scan(_combine, (a, b), reverse=reverse, axis=1)
    return h


def _split_proj(p):
    cuts = [W_CONV, 2 * W_CONV, 3 * W_CONV, 4 * W_CONV, 4 * W_CONV + W_LRU]
    return jnp.split(p, cuts, axis=-1)


def setup_inputs(seed: int = 0) -> dict:
    key = jax.random.key(seed)
    ks = jax.random.split(key, 20)
    f = jnp.float32
    nrm = lambda k, s, sc: jax.random.normal(k, s, f) * sc
    a0 = jax.random.uniform(ks[17], (DEPTH, N_DIR, W_LRU), f, 0.9, 0.999)
    s = a0 ** (1.0 / LRU_C)
    lru_lambda = jnp.log(s) - jnp.log1p(-s)
    return {
        "x": nrm(ks[0], (BATCH, SEQ, D_MODEL), 1.0),
        "c": nrm(ks[1], (BATCH, D_MODEL), 1.0),
        "ctx": nrm(ks[2], (BATCH, CTX_LEN, D_MODEL), 1.0),
        "c_ctx": nrm(ks[3], (D_MODEL,), 1.0),
        "norm_g": 1.0 + nrm(ks[4], (DEPTH, D_MODEL), 0.02),
        "w_ada": nrm(ks[5], (DEPTH, D_MODEL, 3 * D_MODEL), 0.5 * D_MODEL ** -0.5),
        "b_ada": nrm(ks[6], (DEPTH, 3 * D_MODEL), 0.02),
        "w_in": nrm(ks[7], (DEPTH, D_MODEL, D_IN_PROJ), D_MODEL ** -0.5),
        "w_conv_a": nrm(ks[8], (DEPTH, CONV_A_WIDTH, W_CONV), CONV_A_WIDTH ** -0.5),
        "w_conv_b": nrm(ks[9], (DEPTH, CONV_B_WIDTH, W_LRU), CONV_B_WIDTH ** -0.5),
        "b_conv_b": nrm(ks[10], (DEPTH, W_LRU), 0.02),
        "lru_wa": nrm(ks[11], (DEPTH, N_DIR, N_LRU_HEADS, LRU_HEAD_DIM, LRU_HEAD_DIM), LRU_HEAD_DIM ** -0.5),
        "lru_ba": nrm(ks[12], (DEPTH, N_DIR, W_LRU), 0.02),
        "lru_wx": nrm(ks[13], (DEPTH, N_DIR, N_LRU_HEADS, LRU_HEAD_DIM, LRU_HEAD_DIM), LRU_HEAD_DIM ** -0.5),
        "lru_bx": nrm(ks[14], (DEPTH, N_DIR, W_LRU), 0.02),
        "lru_lambda": lru_lambda,
        "w_out": nrm(ks[15], (DEPTH, D_MIX, D_MODEL), D_MIX ** -0.5),
        "final_g": 1.0 + nrm(ks[16], (D_MODEL,), 0.02),
    }


def reference(x, c, ctx, c_ctx, norm_g, w_ada, b_ada, w_in, w_conv_a, w_conv_b, b_conv_b,
              lru_wa, lru_ba, lru_wx, lru_bx, lru_lambda, w_out, final_g):
    h_lat = x
    h_ctx = ctx
    for l in range(DEPTH):
        last = l == DEPTH - 1
        mod_lat = jax.nn.silu(c) @ w_ada[l] + b_ada[l]
        sh_l, sc_l, gt_l = jnp.split(mod_lat, 3, axis=-1)
        mod_ctx = jax.nn.silu(c_ctx) @ w_ada[l] + b_ada[l]
        sh_c, sc_c, gt_c = jnp.split(mod_ctx, 3, axis=-1)

        hl = _rmsnorm(h_lat, norm_g[l]) * (1.0 + sc_l[:, None]) + sh_l[:, None]
        hc = _rmsnorm(h_ctx, norm_g[l]) * (1.0 + sc_c) + sh_c

        bl, cl, ul, gl, vl, ql = _split_proj(hl @ w_in[l])
        bc, cc_, uc, gc, vc, qc = _split_proj(hc @ w_in[l])

        ya = bl * _conv_latent(cl * ul, w_conv_a[l], CONV_A_LEFT) * jax.nn.silu(gl)

        xbl = _conv_latent(vl, w_conv_b[l], CONV_B_LEFT) + b_conv_b[l]
        xbc = _conv_context(vc, w_conv_b[l], CONV_B_LEFT) + b_conv_b[l]
        y_lru = None
        ctx_states = []
        for d, rev in enumerate((False, True)):
            a_c, b_c = _lru_coeffs(xbc, lru_wa[l, d], lru_ba[l, d], lru_wx[l, d], lru_bx[l, d], lru_lambda[l, d])
            hs_c = _linear_scan(a_c, b_c, None, rev)
            h0 = hs_c[:, 0] if rev else hs_c[:, -1]
            a_l, b_l = _lru_coeffs(xbl, lru_wa[l, d], lru_ba[l, d], lru_wx[l, d], lru_bx[l, d], lru_lambda[l, d])
            hs_l = _linear_scan(a_l, b_l, h0, rev)
            y_lru = hs_l if y_lru is None else y_lru + hs_l
            ctx_states.append(hs_c)
        yb = y_lru.astype(h_lat.dtype) * jax.nn.silu(ql)

        out_lat = jnp.concatenate([ya, yb], axis=-1) @ w_out[l]
        new_lat = h_lat + gt_l[:, None] * out_lat

        if not last:
            ya_c = bc * _conv_context(cc_ * uc, w_conv_a[l], CONV_A_LEFT) * jax.nn.silu(gc)
            yb_c = (ctx_states[0] + ctx_states[1]).astype(h_ctx.dtype) * jax.nn.silu(qc)
            out_ctx = jnp.concatenate([ya_c, yb_c], axis=-1) @ w_out[l]
            h_ctx = h_ctx + gt_c * out_ctx
        h_lat = new_lat
    return _rmsnorm(h_lat, final_g)
```

```python
import functools

import jax
import jax.numpy as jnp
from jax import lax
from jax.experimental import pallas as pl
from jax.experimental.pallas import tpu as pltpu

EPS = 1e-6
LRU_C = 8.0
GRID_W = 64
CONV_A_LEFT = 1
CONV_B_LEFT = 2
SUBLANES = 8
MXU_TILE = 256
VMEM_LIMIT = 56 * 1024 * 1024


def _sigmoid(z):
    return jax.nn.sigmoid(z)


def _silu(z):
    return z * _sigmoid(z)


def _rms_mod(x, g_ref, mod_ref, row):
    d = x.shape[-1]
    shift = mod_ref[row:row + 1, 0:d]
    scale = g_ref[...] * (1.0 + mod_ref[row:row + 1, d:2 * d])
    ms = jnp.mean(x * x, axis=-1, keepdims=True)
    return (x * lax.rsqrt(ms + EPS)) * scale + shift


def _dwconv(x, w, left, pos, width):
    t = x.shape[0]
    out = None
    for j in range(w.shape[0]):
        off = j - left
        if off == 0:
            term = x * w[j:j + 1, :]
        else:
            shifted = pltpu.roll(x, (-off) % t, 0)
            valid = (pos + off >= 0) & (pos + off < width)
            term = jnp.where(valid, shifted, 0.0) * w[j:j + 1, :]
        out = term if out is None else out + term
    return out


def _lru_coeffs(xb, wbd_ref, ba, bx, nl8):
    n_groups = wbd_ref.shape[0]
    a_parts, b_parts = [], []
    for g in range(n_groups):
        lo, hi = g * MXU_TILE, (g + 1) * MXU_TILE
        xg = xb[:, lo:hi]
        pre = jnp.dot(xg.astype(jnp.bfloat16), wbd_ref[g], preferred_element_type=jnp.float32)
        r = _sigmoid(pre[:, :MXU_TILE] + ba[:, lo:hi])
        i = _sigmoid(pre[:, MXU_TILE:] + bx[:, lo:hi])
        log_a = r * nl8[:, lo:hi]
        a = jnp.exp(log_a)
        one_minus_a2 = jnp.tanh(-log_a) * (1.0 + a * a)
        a_parts.append(a)
        b_parts.append(jnp.sqrt(one_minus_a2) * (i * xg))
    return jnp.concatenate(a_parts, axis=1), jnp.concatenate(b_parts, axis=1)


def _neg_c_softplus_neg(lam):
    z = -lam
    return -LRU_C * (jnp.maximum(z, 0.0) + jnp.log1p(jnp.exp(-jnp.abs(z))))


def _tile_scan(a, b, reverse):
    sub = lax.broadcasted_iota(jnp.int32, a.shape, 0)
    for d in (1, 2, 4):
        if reverse:
            keep = sub < SUBLANES - d
            shift = SUBLANES - d
        else:
            keep = sub >= d
            shift = d
        a_sh = jnp.where(keep, pltpu.roll(a, shift, 0), 1.0)
        b_sh = jnp.where(keep, pltpu.roll(b, shift, 0), 0.0)
        b = a * b_sh + b
        a = a * a_sh
    return a, b


def _scan_rows(a_ref, b_ref, o_ref, h_f, h_r):
    t = a_ref.shape[1]
    n_tiles = t // SUBLANES
    w = a_ref.shape[2]

    def body(j, carry):
        hf, hr = carry
        rf = pl.multiple_of(j * SUBLANES, SUBLANES)
        rr = pl.multiple_of((n_tiles - 1 - j) * SUBLANES, SUBLANES)
        af, bf = _tile_scan(a_ref[0, pl.ds(rf, SUBLANES), :], b_ref[0, pl.ds(rf, SUBLANES), :], False)
        ar, br = _tile_scan(a_ref[1, pl.ds(rr, SUBLANES), :], b_ref[1, pl.ds(rr, SUBLANES), :], True)
        yf = bf + af * hf
        yr = br + ar * hr
        o_ref[0][pl.ds(rf, SUBLANES), :] = yf
        o_ref[1][pl.ds(rr, SUBLANES), :] = yr
        hf = jnp.broadcast_to(yf[SUBLANES - 1:SUBLANES, :], (SUBLANES, w))
        hr = jnp.broadcast_to(yr[0:1, :], (SUBLANES, w))
        return hf, hr

    return lax.fori_loop(0, n_tiles, body, (h_f, h_r), unroll=2)


def _mod_kernel(cc_ref, w_ref, b_ref, o_ref):
    s = _silu(cc_ref[...])
    o_ref[...] = jnp.dot(s, w_ref[...], preferred_element_type=jnp.float32) + b_ref[...]


def _modulation(cc, w_ada, b_ada):
    rows, d = cc.shape
    n = w_ada.shape[1]
    tn = 1024
    return pl.pallas_call(
        _mod_kernel,
        grid=(n // tn,),
        in_specs=[pl.BlockSpec((rows, d), lambda j: (0, 0)),
                  pl.BlockSpec((d, tn), lambda j: (0, j)),
                  pl.BlockSpec((1, tn), lambda j: (0, j))],
        out_specs=pl.BlockSpec((rows, tn), lambda j: (0, j)),
        out_shape=jax.ShapeDtypeStruct((rows, n), jnp.float32),
        compiler_params=pltpu.CompilerParams(dimension_semantics=("arbitrary",),
                                             vmem_limit_bytes=VMEM_LIMIT),
        name="mod",
    )(cc, w_ada, b_ada)


def _ctx_kernel(x_ref, g_ref, mod_ref, wv_ref, wcb_ref, bcb_ref, wbd_ref, ba_ref, bx_ref, lam_ref,
                h0_ref, a_s, b_s, h_s):
    t = x_ref.shape[0]
    w = wv_ref.shape[1]
    hc = _rms_mod(x_ref[...], g_ref, mod_ref, 1)
    vc = jnp.dot(hc.astype(jnp.bfloat16), wv_ref[...], preferred_element_type=jnp.float32)
    pos = lax.broadcasted_iota(jnp.int32, (t, w), 0)
    xb = _dwconv(vc, wcb_ref[...], CONV_B_LEFT, pos, t) + bcb_ref[...]
    nl8 = _neg_c_softplus_neg(lam_ref[...])
    for d in range(2):
        a, b = _lru_coeffs(xb, wbd_ref.at[d], ba_ref[d:d + 1, :], bx_ref[d:d + 1, :], nl8[d:d + 1, :])
        a_s[d] = a
        b_s[d] = b
    zero = jnp.zeros((SUBLANES, w), jnp.float32)
    hf, hr = _scan_rows(a_s, b_s, (h_s.at[0], h_s.at[1]), zero, zero)
    h0_ref[0:1, :] = hf[0:1, :]
    h0_ref[1:2, :] = hr[0:1, :]


def _context_state(ctx, g, mod, w_in_bf, w_conv_b, b_conv_b, wbd, ba, bx, lam):
    t, d = ctx.shape
    w = w_conv_b.shape[1]
    v_block = 4
    full = lambda shape: pl.BlockSpec(shape, lambda i: (0,) * len(shape))
    return pl.pallas_call(
        _ctx_kernel,
        grid=(1,),
        in_specs=[full((t, d)), full((1, d)), full(mod.shape),
                  pl.BlockSpec((d, w), lambda i: (0, v_block)),
                  full(w_conv_b.shape), full((1, w)), full(wbd.shape), full(ba.shape), full(bx.shape),
                  full(lam.shape)],
        out_specs=full((2, w)),
        out_shape=jax.ShapeDtypeStruct((2, w), jnp.float32),
        scratch_shapes=[pltpu.VMEM((2, t, w), jnp.float32), pltpu.VMEM((2, t, w), jnp.float32),
                        pltpu.VMEM((2, t, w), jnp.float32)],
        compiler_params=pltpu.CompilerParams(dimension_semantics=("arbitrary",),
                                             vmem_limit_bytes=VMEM_LIMIT),
        name="ctx",
    )(ctx, g, mod, w_in_bf, w_conv_b, b_conv_b, wbd, ba, bx, lam)


def _proj_kernel(x_ref, g_ref, mod_ref, w_ref, wca_ref, wcb_ref, bcb_ref, ya_ref, sq_ref, xb_ref):
    t = x_ref.shape[0]
    wdt = ya_ref.shape[1]
    hl = _rms_mod(x_ref[...], g_ref, mod_ref, 0).astype(jnp.bfloat16)
    pos = lax.broadcasted_iota(jnp.int32, (t, MXU_TILE), 0) & (GRID_W - 1)

    def proj(group, lo):
        c0 = group * wdt + lo
        return jnp.dot(hl, w_ref[:, c0:c0 + MXU_TILE], preferred_element_type=jnp.float32)

    for lo in range(0, wdt, MXU_TILE):
        hi = lo + MXU_TILE
        cu = proj(1, lo) * proj(2, lo)
        ya = proj(0, lo) * _dwconv(cu, wca_ref[:, lo:hi], CONV_A_LEFT, pos, GRID_W) * _silu(proj(3, lo))
        ya_ref[:, lo:hi] = ya.astype(ya_ref.dtype)
        xb_ref[:, lo:hi] = _dwconv(proj(4, lo), wcb_ref[:, lo:hi], CONV_B_LEFT, pos, GRID_W) + bcb_ref[:, lo:hi]
        sq_ref[:, lo:hi] = _silu(proj(5, lo))


def _input_projection(x, g, mod, w_in_bf, w_conv_a, w_conv_b, b_conv_b, block_t):
    l, d = x.shape
    w = w_conv_a.shape[1]
    assert block_t % GRID_W == 0 and GRID_W & (GRID_W - 1) == 0
    const = lambda shape: pl.BlockSpec(shape, lambda i: (0,) * len(shape))
    tok = lambda width: pl.BlockSpec((block_t, width), lambda i: (i, 0))
    return pl.pallas_call(
        _proj_kernel,
        grid=(l // block_t,),
        in_specs=[tok(d), const((1, d)), const(mod.shape),
                  pl.BlockSpec(w_in_bf.shape, lambda i: (0, 0), pipeline_mode=pl.Buffered(1)),
                  const(w_conv_a.shape), const(w_conv_b.shape), const((1, w))],
        out_specs=[tok(w), tok(w), tok(w)],
        out_shape=[jax.ShapeDtypeStruct((l, w), jnp.bfloat16),
                   jax.ShapeDtypeStruct((l, w), jnp.float32),
                   jax.ShapeDtypeStruct((l, w), jnp.float32)],
        compiler_params=pltpu.CompilerParams(dimension_semantics=("arbitrary",),
                                             vmem_limit_bytes=VMEM_LIMIT),
        name="proj",
    )(x, g, mod, w_in_bf, w_conv_a, w_conv_b, b_conv_b)


def _scan_kernel(xf_ref, xr_ref, h0_ref, wbd_ref, ba_ref, bx_ref, lam_ref, hf_ref, hr_ref, a_s, b_s, carry_s):
    w = xf_ref.shape[1]

    @pl.when(pl.program_id(0) == 0)
    def _():
        carry_s[0] = jnp.broadcast_to(h0_ref[0:1, :], (SUBLANES, w))
        carry_s[1] = jnp.broadcast_to(h0_ref[1:2, :], (SUBLANES, w))

    nl8 = _neg_c_softplus_neg(lam_ref[...])
    for d, x_ref in ((0, xf_ref), (1, xr_ref)):
        a, b = _lru_coeffs(x_ref[...], wbd_ref.at[d], ba_ref[d:d + 1, :], bx_ref[d:d + 1, :], nl8[d:d + 1, :])
        a_s[d] = a
        b_s[d] = b
    hf, hr = _scan_rows(a_s, b_s, (hf_ref, hr_ref), carry_s[0], carry_s[1])
    carry_s[0] = hf
    carry_s[1] = hr


def _bidirectional_scan(xb, h0, wbd, ba, bx, lam, block_t):
    l, w = xb.shape
    nb = l // block_t
    const = lambda shape: pl.BlockSpec(shape, lambda k: (0,) * len(shape))
    fwd = pl.BlockSpec((block_t, w), lambda k: (k, 0))
    rev = pl.BlockSpec((block_t, w), lambda k: (nb - 1 - k, 0))
    return pl.pallas_call(
        _scan_kernel,
        grid=(nb,),
        in_specs=[fwd, rev, const((2, w)), const(wbd.shape), const(ba.shape), const(bx.shape), const(lam.shape)],
        out_specs=[fwd, rev],
        out_shape=[jax.ShapeDtypeStruct((l, w), jnp.float32), jax.ShapeDtypeStruct((l, w), jnp.float32)],
        scratch_shapes=[pltpu.VMEM((2, block_t, w), jnp.float32), pltpu.VMEM((2, block_t, w), jnp.float32),
                        pltpu.VMEM((2, SUBLANES, w), jnp.float32)],
        compiler_params=pltpu.CompilerParams(dimension_semantics=("arbitrary",),
                                             vmem_limit_bytes=VMEM_LIMIT),
        name="scan",
    )(xb, xb, h0, wbd, ba, bx, lam)


def _out_kernel(x_ref, ya_ref, hf_ref, hr_ref, sq_ref, w_ref, mod_ref, g_ref, o_ref):
    wdt = ya_ref.shape[1]
    d = x_ref.shape[1]
    yb = ((hf_ref[...] + hr_ref[...]) * sq_ref[...]).astype(jnp.bfloat16)
    out = jnp.dot(ya_ref[...], w_ref[0:wdt, :], preferred_element_type=jnp.float32)
    out = out + jnp.dot(yb, w_ref[wdt:2 * wdt, :], preferred_element_type=jnp.float32)
    new = x_ref[...] + mod_ref[0:1, 2 * d:3 * d] * out
    ms = jnp.mean(new * new, axis=-1, keepdims=True)
    o_ref[...] = (new * lax.rsqrt(ms + EPS)) * g_ref[...]


def _output_projection(x, ya, hf, hr, sq, w_out_bf, mod, final_g, block_t):
    l, d = x.shape
    w = ya.shape[1]
    const = lambda shape: pl.BlockSpec(shape, lambda i: (0,) * len(shape))
    tok = lambda width: pl.BlockSpec((block_t, width), lambda i: (i, 0))
    return pl.pallas_call(
        _out_kernel,
        grid=(l // block_t,),
        in_specs=[tok(d), tok(w), tok(w), tok(w), tok(w),
                  pl.BlockSpec(w_out_bf.shape, lambda i: (0, 0), pipeline_mode=pl.Buffered(1)),
                  const(mod.shape), const((1, d))],
        out_specs=tok(d),
        out_shape=jax.ShapeDtypeStruct((l, d), jnp.float32),
        compiler_params=pltpu.CompilerParams(dimension_semantics=("arbitrary",),
                                             vmem_limit_bytes=VMEM_LIMIT),
        name="out",
    )(x, ya, hf, hr, sq, w_out_bf, mod, final_g)


def _block_diag_groups(w):
    n_dir, n_heads, dh, _ = w.shape
    per = MXU_TILE // dh
    w5 = w.reshape(n_dir, n_heads // per, per, dh, dh)
    eye = jnp.eye(per, dtype=w.dtype)
    bd = w5[:, :, :, :, None, :] * eye[None, None, :, None, :, None]
    return bd.reshape(n_dir, n_heads // per, MXU_TILE, MXU_TILE)


def kernel(x, c, ctx, c_ctx, norm_g, w_ada, b_ada, w_in, w_conv_a, w_conv_b, b_conv_b,
           lru_wa, lru_ba, lru_wx, lru_bx, lru_lambda, w_out, final_g):
    assert w_in.shape[0] == 1 and x.shape[0] == 1, "single layer, single batch element"
    d = x.shape[-1]
    block_t = 512

    cc = jnp.concatenate([c, c_ctx[None, :], jnp.zeros((SUBLANES - 2, d), x.dtype)], axis=0)
    mod = _modulation(cc, w_ada[0], b_ada)
    g = norm_g

    w_in_bf = w_in[0].astype(jnp.bfloat16)
    w_out_bf = w_out[0].astype(jnp.bfloat16)
    wbd = jnp.concatenate([_block_diag_groups(lru_wa[0]), _block_diag_groups(lru_wx[0])],
                          axis=-1).astype(jnp.bfloat16)
    ba, bx, lam = lru_ba[0], lru_bx[0], lru_lambda[0]

    h0 = _context_state(ctx[0], g, mod, w_in_bf, w_conv_b[0], b_conv_b, wbd, ba, bx, lam)
    ya, sq, xb = _input_projection(x[0], g, mod, w_in_bf, w_conv_a[0], w_conv_b[0], b_conv_b, block_t)
    hf, hr = _bidirectional_scan(xb, h0, wbd, ba, bx, lam, block_t)
    out = _output_projection(x[0], ya, hf, hr, sq, w_out_bf, mod, final_g[None, :], block_t)
    return out[None]
```

```python
import jax
import jax.numpy as jnp
from jax import lax
from jax.experimental import pallas as pl
from jax.experimental.pallas import tpu as pltpu

EPS = 1e-6
LRU_C = 8.0
GRID_W = 64
CONV_A_LEFT = 1
CONV_B_LEFT = 2
SUBLANES = 8
LANES = 128
MXU_TILE = 256
N_SEG = SUBLANES
VMEM_LIMIT = 60 * 1024 * 1024


def _sigmoid(z):
    return 0.5 * jnp.tanh(0.5 * z) + 0.5


def _silu(z):
    return z * _sigmoid(z)


def _rms_mod(x, g_ref, mod_ref, row):
    d = x.shape[-1]
    shift = mod_ref[row:row + 1, 0:d]
    scale = g_ref[...] * (1.0 + mod_ref[row:row + 1, d:2 * d])
    ms = jnp.mean(x * x, axis=-1, keepdims=True)
    return (x * lax.rsqrt(ms + EPS)) * scale + shift


def _c_softplus_neg(lam):
    z = -lam
    return LRU_C * (jnp.maximum(z, 0.0) + jnp.log1p(jnp.exp(-jnp.abs(z))))


def _lru_coeffs(xg, wbd, ba, bx, c8):
    pre = jnp.dot(xg.astype(jnp.bfloat16), wbd, preferred_element_type=jnp.float32)
    r = _sigmoid(pre[:, :MXU_TILE] + ba)
    i = _sigmoid(pre[:, MXU_TILE:] + bx)
    u = r * c8
    a = jnp.exp(-u)
    one_minus_a2 = jnp.tanh(u) * (1.0 + a * a)
    return a, jnp.sqrt(one_minus_a2) * (i * xg)


def _mod_kernel(cc_ref, w_ref, b_ref, o_ref):
    s = _silu(cc_ref[...])
    o_ref[...] = jnp.dot(s, w_ref[...], preferred_element_type=jnp.float32) + b_ref[...]


def _modulation(cc, w_ada, b_ada):
    rows, d = cc.shape
    n = w_ada.shape[1]
    tn = 1024
    return pl.pallas_call(
        _mod_kernel,
        grid=(n // tn,),
        in_specs=[pl.BlockSpec((rows, d), lambda j: (0, 0)),
                  pl.BlockSpec((d, tn), lambda j: (0, j)),
                  pl.BlockSpec((1, tn), lambda j: (0, j))],
        out_specs=pl.BlockSpec((rows, tn), lambda j: (0, j)),
        out_shape=jax.ShapeDtypeStruct((rows, n), jnp.float32),
        compiler_params=pltpu.CompilerParams(dimension_semantics=("arbitrary",),
                                             vmem_limit_bytes=VMEM_LIMIT),
        name="mod",
    )(cc, w_ada, b_ada)


def _dwconv_seq(x, w, left):
    t = x.shape[0]
    pos = lax.broadcasted_iota(jnp.int32, x.shape, 0)
    out = None
    for j in range(w.shape[0]):
        off = j - left
        if off == 0:
            term = x * w[j:j + 1, :]
        else:
            shifted = pltpu.roll(x, (-off) % t, 0)
            valid = (pos + off >= 0) & (pos + off < t)
            term = jnp.where(valid, shifted, 0.0) * w[j:j + 1, :]
        out = term if out is None else out + term
    return out


def _tile_scan(a, b, reverse):
    sub = lax.broadcasted_iota(jnp.int32, a.shape, 0)
    for d in (1, 2, 4):
        if reverse:
            keep = sub < SUBLANES - d
            shift = SUBLANES - d
        else:
            keep = sub >= d
            shift = d
        a_sh = jnp.where(keep, pltpu.roll(a, shift, 0), 1.0)
        b_sh = jnp.where(keep, pltpu.roll(b, shift, 0), 0.0)
        b = a * b_sh + b
        a = a * a_sh
    return a, b


def _final_states(a_ref, b_ref):
    t = a_ref.shape[1]
    n_tiles = t // SUBLANES
    w = a_ref.shape[2]

    def body(j, carry):
        hf, hr = carry
        rf = pl.multiple_of(j * SUBLANES, SUBLANES)
        rr = pl.multiple_of((n_tiles - 1 - j) * SUBLANES, SUBLANES)
        af, bf = _tile_scan(a_ref[0, pl.ds(rf, SUBLANES), :], b_ref[0, pl.ds(rf, SUBLANES), :], False)
        ar, br = _tile_scan(a_ref[1, pl.ds(rr, SUBLANES), :], b_ref[1, pl.ds(rr, SUBLANES), :], True)
        yf = bf + af * hf
        yr = br + ar * hr
        hf = jnp.broadcast_to(yf[SUBLANES - 1:SUBLANES, :], (SUBLANES, w))
        hr = jnp.broadcast_to(yr[0:1, :], (SUBLANES, w))
        return hf, hr

    zero = jnp.zeros((SUBLANES, w), jnp.float32)
    return lax.fori_loop(0, n_tiles, body, (zero, zero), unroll=2)


def _ctx_kernel(x_ref, g_ref, mod_ref, wv_ref, wcb_ref, bcb_ref, wbd_ref, ba_ref, bx_ref, lam_ref,
                h0_ref, a_s, b_s):
    w = wv_ref.shape[1]
    hc = _rms_mod(x_ref[...], g_ref, mod_ref, 1)
    vc = jnp.dot(hc.astype(jnp.bfloat16), wv_ref[...], preferred_element_type=jnp.float32)
    xb = _dwconv_seq(vc, wcb_ref[...], CONV_B_LEFT) + bcb_ref[...]
    c8 = _c_softplus_neg(lam_ref[...])
    for d in range(2):
        for g in range(w // MXU_TILE):
            lo, hi = g * MXU_TILE, (g + 1) * MXU_TILE
            a, b = _lru_coeffs(xb[:, lo:hi], wbd_ref[d, g], ba_ref[d:d + 1, lo:hi], bx_ref[d:d + 1, lo:hi],
                               c8[d:d + 1, lo:hi])
            a_s[d, :, lo:hi] = a
            b_s[d, :, lo:hi] = b
    hf, hr = _final_states(a_s, b_s)
    h0_ref[0:1, :] = hf[0:1, :]
    h0_ref[1:2, :] = hr[0:1, :]


def _context_state(ctx, g, mod, w_in_bf, w_conv_b, b_conv_b, wbd, ba, bx, lam):
    t, d = ctx.shape
    w = w_conv_b.shape[1]
    v_block = 4
    full = lambda shape: pl.BlockSpec(shape, lambda i: (0,) * len(shape))
    return pl.pallas_call(
        _ctx_kernel,
        grid=(1,),
        in_specs=[full((t, d)), full((1, d)), full(mod.shape),
                  pl.BlockSpec((d, w), lambda i: (0, v_block)),
                  full(w_conv_b.shape), full((1, w)), full(wbd.shape), full(ba.shape), full(bx.shape),
                  full(lam.shape)],
        out_specs=full((2, w)),
        out_shape=jax.ShapeDtypeStruct((2, w), jnp.float32),
        scratch_shapes=[pltpu.VMEM((2, t, w), jnp.float32), pltpu.VMEM((2, t, w), jnp.float32)],
        compiler_params=pltpu.CompilerParams(dimension_semantics=("arbitrary",),
                                             vmem_limit_bytes=VMEM_LIMIT),
        name="ctx",
    )(ctx, g, mod, w_in_bf, w_conv_b, b_conv_b, wbd, ba, bx, lam)


def _shift_tokens(x, off):
    if off == 0:
        return x
    pad = jnp.zeros((abs(off) * N_SEG, x.shape[1]), x.dtype)
    if off < 0:
        return jnp.concatenate([pad, x[:off * N_SEG]], axis=0)
    return jnp.concatenate([x[off * N_SEG:], pad], axis=0)


def _dwconv_row(x, w, left):
    out = None
    for j in range(w.shape[0]):
        term = _shift_tokens(x, j - left) * w[j:j + 1, :]
        out = term if out is None else out + term
    return out


def _local_scan(a, b, reverse):
    n = a.shape[0] // N_SEG
    order = range(n - 1, -1, -1) if reverse else range(n)
    hs, ps = [None] * n, [None] * n
    h = p = None
    for t in order:
        at, bt = a[t * N_SEG:(t + 1) * N_SEG], b[t * N_SEG:(t + 1) * N_SEG]
        h = bt if h is None else at * h + bt
        p = at if p is None else at * p
        hs[t], ps[t] = h, p
    return jnp.concatenate(hs, axis=0), jnp.concatenate(ps, axis=0), h, p


def _proj_kernel(x_ref, g_ref, mod_ref, w_ref, wca_ref, wcb_ref, bcb_ref, wbd_ref, ba_ref, bx_ref, lam_ref,
                 ya_ref, s_ref, pf_ref, pr_ref, summ_ref, hi_s):
    n_seg, ti, d = x_ref.shape
    wdt = ya_ref.shape[1]
    n_slab = d // LANES

    for s in range(n_seg):
        hs = _rms_mod(x_ref[s], g_ref, mod_ref, 0)
        for c in range(n_slab):
            hi_s[c, pl.ds(s, ti, stride=n_seg), :] = hs[:, c * LANES:(c + 1) * LANES]
    hl = jnp.concatenate([hi_s[c] for c in range(n_slab)], axis=1).astype(jnp.bfloat16)

    def proj(group, lo):
        c0 = group * wdt + lo
        return jnp.dot(hl, w_ref[:, c0:c0 + MXU_TILE], preferred_element_type=jnp.float32)

    c8 = _c_softplus_neg(lam_ref[...])
    for g in range(wdt // MXU_TILE):
        lo, hi = g * MXU_TILE, (g + 1) * MXU_TILE
        cu = proj(1, lo) * proj(2, lo)
        ya = proj(0, lo) * _dwconv_row(cu, wca_ref[:, lo:hi], CONV_A_LEFT) * _silu(proj(3, lo))
        ya_ref[:, lo:hi] = ya.astype(ya_ref.dtype)
        xb = _dwconv_row(proj(4, lo), wcb_ref[:, lo:hi], CONV_B_LEFT) + bcb_ref[:, lo:hi]
        sq = _silu(proj(5, lo))
        af, bf = _lru_coeffs(xb, wbd_ref[0, g], ba_ref[0:1, lo:hi], bx_ref[0:1, lo:hi], c8[0:1, lo:hi])
        hf, pf, hf_end, pf_end = _local_scan(af, bf, False)
        ar, br = _lru_coeffs(xb, wbd_ref[1, g], ba_ref[1:2, lo:hi], bx_ref[1:2, lo:hi], c8[1:2, lo:hi])
        hr, pr, hr_end, pr_end = _local_scan(ar, br, True)
        s_ref[:, lo:hi] = (hf + hr) * sq
        pf_ref[:, lo:hi] = pf * sq
        pr_ref[:, lo:hi] = pr * sq
        summ_ref[0, 0, :, lo:hi] = hf_end
        summ_ref[0, 1, :, lo:hi] = pf_end
        summ_ref[0, 2, :, lo:hi] = hr_end
        summ_ref[0, 3, :, lo:hi] = pr_end


def _input_projection(x3, g, mod, w_in_bf, w_conv_a, w_conv_b, b_conv_b, wbd, ba, bx, lam):
    n_seg, seg, d = x3.shape
    w = w_conv_a.shape[1]
    ti = GRID_W
    tb = ti * n_seg
    nb = seg // ti
    const = lambda shape: pl.BlockSpec(shape, lambda k: (0,) * len(shape))
    rows = lambda: pl.BlockSpec((tb, w), lambda k: (k, 0))
    f32 = jnp.float32
    return pl.pallas_call(
        _proj_kernel,
        grid=(nb,),
        in_specs=[pl.BlockSpec((n_seg, ti, d), lambda k: (0, k, 0)), const((1, d)), const(mod.shape),
                  pl.BlockSpec(w_in_bf.shape, lambda k: (0, 0), pipeline_mode=pl.Buffered(1)),
                  const(w_conv_a.shape), const(w_conv_b.shape), const((1, w)),
                  const(wbd.shape), const(ba.shape), const(bx.shape), const(lam.shape)],
        out_specs=[rows(), rows(), rows(), rows(),
                   pl.BlockSpec((1, 4, n_seg, w), lambda k: (k, 0, 0, 0))],
        out_shape=[jax.ShapeDtypeStruct((nb * tb, w), jnp.bfloat16),
                   jax.ShapeDtypeStruct((nb * tb, w), f32),
                   jax.ShapeDtypeStruct((nb * tb, w), f32),
                   jax.ShapeDtypeStruct((nb * tb, w), f32),
                   jax.ShapeDtypeStruct((nb, 4, n_seg, w), f32)],
        scratch_shapes=[pltpu.VMEM((d // LANES, tb, LANES), f32)],
        compiler_params=pltpu.CompilerParams(dimension_semantics=("arbitrary",),
                                             vmem_limit_bytes=VMEM_LIMIT),
        name="proj",
    )(x3, g, mod, w_in_bf, w_conv_a, w_conv_b, b_conv_b, wbd, ba, bx, lam)


def _carry_kernel(summ_ref, h0_ref, cf_ref, cr_ref):
    nb, _, n_seg, _ = summ_ref.shape
    c = h0_ref[0:1, :]
    for s in range(n_seg):
        for k in range(nb):
            cf_ref[k, s:s + 1, :] = c
            c = summ_ref[k, 0, s:s + 1, :] + summ_ref[k, 1, s:s + 1, :] * c
    c = h0_ref[1:2, :]
    for s in range(n_seg - 1, -1, -1):
        for k in range(nb - 1, -1, -1):
            cr_ref[k, s:s + 1, :] = c
            c = summ_ref[k, 2, s:s + 1, :] + summ_ref[k, 3, s:s + 1, :] * c


def _chunk_carries(summ, h0):
    nb, _, n_seg, w = summ.shape
    full = lambda shape: pl.BlockSpec(shape, lambda i: (0,) * len(shape))
    return pl.pallas_call(
        _carry_kernel,
        grid=(1,),
        in_specs=[full(summ.shape), full(h0.shape)],
        out_specs=[full((nb, n_seg, w)), full((nb, n_seg, w))],
        out_shape=[jax.ShapeDtypeStruct((nb, n_seg, w), jnp.float32)] * 2,
        compiler_params=pltpu.CompilerParams(dimension_semantics=("arbitrary",)),
        name="carry",
    )(summ, h0)


def _out_kernel(x_ref, ya_ref, s_ref, pf_ref, pr_ref, cf_ref, cr_ref, w_ref, mod_ref, g_ref, o_ref, r_s):
    n_seg, ti, d = x_ref.shape
    wdt = ya_ref.shape[1]
    n_slab = d // LANES
    cf = jnp.tile(cf_ref[0], (ti, 1))
    cr = jnp.tile(cr_ref[0], (ti, 1))
    yb = (s_ref[...] + pf_ref[...] * cf + pr_ref[...] * cr).astype(jnp.bfloat16)
    out = jnp.dot(ya_ref[...], w_ref[0:wdt, :], preferred_element_type=jnp.float32)
    out = out + jnp.dot(yb, w_ref[wdt:2 * wdt, :], preferred_element_type=jnp.float32)
    for c in range(n_slab):
        r_s[c] = out[:, c * LANES:(c + 1) * LANES]
    gate = mod_ref[0:1, 2 * d:3 * d]
    for s in range(n_seg):
        out_s = jnp.concatenate([r_s[c, pl.ds(s, ti, stride=n_seg), :] for c in range(n_slab)], axis=1)
        new = x_ref[s] + gate * out_s
        ms = jnp.mean(new * new, axis=-1, keepdims=True)
        o_ref[s] = (new * lax.rsqrt(ms + EPS)) * g_ref[...]


def _output_projection(x3, ya, s, pf, pr, cf, cr, w_out_bf, mod, final_g):
    n_seg, seg, d = x3.shape
    w = ya.shape[1]
    ti = GRID_W
    tb = ti * n_seg
    nb = seg // ti
    const = lambda shape: pl.BlockSpec(shape, lambda k: (0,) * len(shape))
    rows = lambda: pl.BlockSpec((tb, w), lambda k: (k, 0))
    segs = lambda: pl.BlockSpec((n_seg, ti, d), lambda k: (0, k, 0))
    carry = lambda: pl.BlockSpec((1, n_seg, w), lambda k: (k, 0, 0))
    return pl.pallas_call(
        _out_kernel,
        grid=(nb,),
        in_specs=[segs(), rows(), rows(), rows(), rows(), carry(), carry(),
                  pl.BlockSpec(w_out_bf.shape, lambda k: (0, 0), pipeline_mode=pl.Buffered(1)),
                  const(mod.shape), const((1, d))],
        out_specs=segs(),
        out_shape=jax.ShapeDtypeStruct((n_seg, seg, d), jnp.float32),
        scratch_shapes=[pltpu.VMEM((d // LANES, tb, LANES), jnp.float32)],
        compiler_params=pltpu.CompilerParams(dimension_semantics=("arbitrary",),
                                             vmem_limit_bytes=VMEM_LIMIT),
        name="out",
    )(x3, ya, s, pf, pr, cf, cr, w_out_bf, mod, final_g)


def _block_diag_groups(w):
    n_dir, n_heads, dh, _ = w.shape
    per = MXU_TILE // dh
    w5 = w.reshape(n_dir, n_heads // per, per, dh, dh)
    eye = jnp.eye(per, dtype=w.dtype)
    bd = w5[:, :, :, :, None, :] * eye[None, None, :, None, :, None]
    return bd.reshape(n_dir, n_heads // per, MXU_TILE, MXU_TILE)


def kernel(x, c, ctx, c_ctx, norm_g, w_ada, b_ada, w_in, w_conv_a, w_conv_b, b_conv_b,
           lru_wa, lru_ba, lru_wx, lru_bx, lru_lambda, w_out, final_g):
    assert w_in.shape[0] == 1 and x.shape[0] == 1, "single layer, single batch element"
    _, l, d = x.shape
    assert l % (N_SEG * GRID_W) == 0

    cc = jnp.concatenate([c, c_ctx[None, :], jnp.zeros((SUBLANES - 2, d), x.dtype)], axis=0)
    mod = _modulation(cc, w_ada[0], b_ada)
    g = norm_g

    w_in_bf = w_in[0].astype(jnp.bfloat16)
    w_out_bf = w_out[0].astype(jnp.bfloat16)
    wbd = jnp.concatenate([_block_diag_groups(lru_wa[0]), _block_diag_groups(lru_wx[0])],
                          axis=-1).astype(jnp.bfloat16)
    ba, bx, lam = lru_ba[0], lru_bx[0], lru_lambda[0]

    x3 = x[0].reshape(N_SEG, l // N_SEG, d)
    h0 = _context_state(ctx[0], g, mod, w_in_bf, w_conv_b[0], b_conv_b, wbd, ba, bx, lam)
    ya, s, pf, pr, summ = _input_projection(x3, g, mod, w_in_bf, w_conv_a[0], w_conv_b[0], b_conv_b,
                                            wbd, ba, bx, lam)
    cf, cr = _chunk_carries(summ, h0)
    out = _output_projection(x3, ya, s, pf, pr, cf, cr, w_out_bf, mod, final_g[None, :])
    return out.reshape(1, l, d)
```

```python
import jax
import jax.numpy as jnp
from jax import lax
from jax.experimental import pallas as pl
from jax.experimental.pallas import tpu as pltpu

EPS = 1e-6
LRU_C = 8.0
GRID_W = 64
CONV_A_LEFT = 1
CONV_B_LEFT = 2
SUBLANES = 8
LANES = 128
MXU_TILE = 256
N_SEG = SUBLANES
VMEM_LIMIT = 60 * 1024 * 1024


def _sigmoid(z):
    return 0.5 * jnp.tanh(0.5 * z) + 0.5


def _silu(z):
    return z * _sigmoid(z)


def _rms_mod(x, g_ref, mod_ref, row):
    d = x.shape[-1]
    shift = mod_ref[row:row + 1, 0:d]
    scale = g_ref[...] * (1.0 + mod_ref[row:row + 1, d:2 * d])
    ms = jnp.mean(x * x, axis=-1, keepdims=True)
    return (x * lax.rsqrt(ms + EPS)) * scale + shift


def _c_softplus_neg(lam):
    z = -lam
    return LRU_C * (jnp.maximum(z, 0.0) + jnp.log1p(jnp.exp(-jnp.abs(z))))


def _gate_preact(xg, wbd):
    return jnp.dot(xg.astype(jnp.bfloat16), wbd, preferred_element_type=jnp.float32)


def _lru_coeffs(xg, pre_r, pre_i, ba, bx, c8):
    r = _sigmoid(pre_r + ba)
    i = _sigmoid(pre_i + bx)
    u = r * c8
    a = jnp.exp(-u)
    one_minus_a2 = jnp.tanh(u) * (1.0 + a * a)
    return a, jnp.sqrt(one_minus_a2) * (i * xg)


def _mod_kernel(cc_ref, w_ref, b_ref, o_ref):
    s = _silu(cc_ref[...])
    o_ref[...] = jnp.dot(s, w_ref[...], preferred_element_type=jnp.float32) + b_ref[...]


def _modulation(cc, w_ada, b_ada):
    rows, d = cc.shape
    n = w_ada.shape[1]
    tn = 1024
    return pl.pallas_call(
        _mod_kernel,
        grid=(n // tn,),
        in_specs=[pl.BlockSpec((rows, d), lambda j: (0, 0)),
                  pl.BlockSpec((d, tn), lambda j: (0, j)),
                  pl.BlockSpec((1, tn), lambda j: (0, j))],
        out_specs=pl.BlockSpec((rows, tn), lambda j: (0, j)),
        out_shape=jax.ShapeDtypeStruct((rows, n), jnp.float32),
        compiler_params=pltpu.CompilerParams(dimension_semantics=("arbitrary",),
                                             vmem_limit_bytes=VMEM_LIMIT),
        name="mod",
    )(cc, w_ada, b_ada)


def _dwconv_seq(x, w, left):
    t = x.shape[0]
    pos = lax.broadcasted_iota(jnp.int32, x.shape, 0)
    out = None
    for j in range(w.shape[0]):
        off = j - left
        if off == 0:
            term = x * w[j:j + 1, :]
        else:
            shifted = pltpu.roll(x, (-off) % t, 0)
            valid = (pos + off >= 0) & (pos + off < t)
            term = jnp.where(valid, shifted, 0.0) * w[j:j + 1, :]
        out = term if out is None else out + term
    return out


def _tile_scan(a, b, reverse):
    sub = lax.broadcasted_iota(jnp.int32, a.shape, 0)
    for d in (1, 2, 4):
        if reverse:
            keep = sub < SUBLANES - d
            shift = SUBLANES - d
        else:
            keep = sub >= d
            shift = d
        a_sh = jnp.where(keep, pltpu.roll(a, shift, 0), 1.0)
        b_sh = jnp.where(keep, pltpu.roll(b, shift, 0), 0.0)
        b = a * b_sh + b
        a = a * a_sh
    return a, b


def _final_states(a_ref, b_ref):
    t = a_ref.shape[1]
    n_tiles = t // SUBLANES
    w = a_ref.shape[2]

    def body(j, carry):
        hf, hr = carry
        rf = pl.multiple_of(j * SUBLANES, SUBLANES)
        rr = pl.multiple_of((n_tiles - 1 - j) * SUBLANES, SUBLANES)
        af, bf = _tile_scan(a_ref[0, pl.ds(rf, SUBLANES), :], b_ref[0, pl.ds(rf, SUBLANES), :], False)
        ar, br = _tile_scan(a_ref[1, pl.ds(rr, SUBLANES), :], b_ref[1, pl.ds(rr, SUBLANES), :], True)
        yf = bf + af * hf
        yr = br + ar * hr
        hf = jnp.broadcast_to(yf[SUBLANES - 1:SUBLANES, :], (SUBLANES, w))
        hr = jnp.broadcast_to(yr[0:1, :], (SUBLANES, w))
        return hf, hr

    zero = jnp.zeros((SUBLANES, w), jnp.float32)
    return lax.fori_loop(0, n_tiles, body, (zero, zero), unroll=2)


def _ctx_kernel(x_ref, g_ref, mod_ref, wv_ref, wcb_ref, bcb_ref, wbd_ref, ba_ref, bx_ref, lam_ref,
                h0_ref, a_s, b_s):
    w = wv_ref.shape[1]
    hc = _rms_mod(x_ref[...], g_ref, mod_ref, 1)
    vc = jnp.dot(hc.astype(jnp.bfloat16), wv_ref[...], preferred_element_type=jnp.float32)
    xb = _dwconv_seq(vc, wcb_ref[...], CONV_B_LEFT) + bcb_ref[...]
    c8 = _c_softplus_neg(lam_ref[...])
    for d in range(2):
        for g in range(w // MXU_TILE):
            lo, hi = g * MXU_TILE, (g + 1) * MXU_TILE
            pre = _gate_preact(xb[:, lo:hi], wbd_ref[d, g])
            a, b = _lru_coeffs(xb[:, lo:hi], pre[:, :MXU_TILE], pre[:, MXU_TILE:],
                               ba_ref[d:d + 1, lo:hi], bx_ref[d:d + 1, lo:hi], c8[d:d + 1, lo:hi])
            a_s[d, :, lo:hi] = a
            b_s[d, :, lo:hi] = b
    hf, hr = _final_states(a_s, b_s)
    h0_ref[0:1, :] = hf[0:1, :]
    h0_ref[1:2, :] = hr[0:1, :]


def _context_state(ctx, g, mod, w_in_bf, w_conv_b, b_conv_b, wbd, ba, bx, lam):
    t, d = ctx.shape
    w = w_conv_b.shape[1]
    v_block = 4
    full = lambda shape: pl.BlockSpec(shape, lambda i: (0,) * len(shape))
    return pl.pallas_call(
        _ctx_kernel,
        grid=(1,),
        in_specs=[full((t, d)), full((1, d)), full(mod.shape),
                  pl.BlockSpec((d, w), lambda i: (0, v_block)),
                  full(w_conv_b.shape), full((1, w)), full(wbd.shape), full(ba.shape), full(bx.shape),
                  full(lam.shape)],
        out_specs=full((2, w)),
        out_shape=jax.ShapeDtypeStruct((2, w), jnp.float32),
        scratch_shapes=[pltpu.VMEM((2, t, w), jnp.float32), pltpu.VMEM((2, t, w), jnp.float32)],
        compiler_params=pltpu.CompilerParams(dimension_semantics=("arbitrary",),
                                             vmem_limit_bytes=VMEM_LIMIT),
        name="ctx",
    )(ctx, g, mod, w_in_bf, w_conv_b, b_conv_b, wbd, ba, bx, lam)


def _shift_tokens(x, off):
    if off == 0:
        return x
    pad = jnp.zeros((abs(off) * N_SEG, x.shape[1]), x.dtype)
    if off < 0:
        return jnp.concatenate([pad, x[:off * N_SEG]], axis=0)
    return jnp.concatenate([x[off * N_SEG:], pad], axis=0)


def _dwconv_row(x, w, left):
    out = None
    for j in range(w.shape[0]):
        term = _shift_tokens(x, j - left) * w[j:j + 1, :]
        out = term if out is None else out + term
    return out


def _local_scan(a, b, reverse):
    n = a.shape[0] // N_SEG
    order = range(n - 1, -1, -1) if reverse else range(n)
    hs, ps = [None] * n, [None] * n
    h = p = None
    for t in order:
        at, bt = a[t * N_SEG:(t + 1) * N_SEG], b[t * N_SEG:(t + 1) * N_SEG]
        h = bt if h is None else at * h + bt
        p = at if p is None else at * p
        hs[t], ps[t] = h, p
    return jnp.concatenate(hs, axis=0), jnp.concatenate(ps, axis=0), h, p


def _proj_kernel(x_ref, g_ref, mod_ref, w_ref, wca_ref, wcb_ref, bcb_ref, wbd_ref, ba_ref, bx_ref, lam_ref,
                 ya_ref, s_ref, pf_ref, pr_ref, summ_ref, hi_s):
    n_seg, ti, d = x_ref.shape
    wdt = ya_ref.shape[1]
    n_slab = d // LANES

    for s in range(n_seg):
        hs = _rms_mod(x_ref[s], g_ref, mod_ref, 0)
        for c in range(n_slab):
            hi_s[c, pl.ds(s, ti, stride=n_seg), :] = hs[:, c * LANES:(c + 1) * LANES]
    hl = jnp.concatenate([hi_s[c] for c in range(n_slab)], axis=1).astype(jnp.bfloat16)

    def proj(group, lo):
        c0 = group * wdt + lo
        return jnp.dot(hl, w_ref[:, c0:c0 + MXU_TILE], preferred_element_type=jnp.float32)

    gate_slabs = 2 * MXU_TILE // LANES
    assert n_slab >= 4 * gate_slabs

    def park(slab0, pre):
        for j in range(gate_slabs):
            hi_s[slab0 + j] = pre[:, j * LANES:(j + 1) * LANES]

    def unpark(slab0, half):
        j0 = slab0 + half * (gate_slabs // 2)
        return jnp.concatenate([hi_s[j0 + j] for j in range(gate_slabs // 2)], axis=1)

    c8 = _c_softplus_neg(lam_ref[...])
    n_groups = wdt // MXU_TILE
    v_next = proj(4, 0)
    for g in range(n_groups):
        lo, hi = g * MXU_TILE, (g + 1) * MXU_TILE
        slab_f = (g % 2) * 2 * gate_slabs
        slab_r = slab_f + gate_slabs
        xb = _dwconv_row(v_next, wcb_ref[:, lo:hi], CONV_B_LEFT) + bcb_ref[:, lo:hi]
        xb16 = xb.astype(jnp.bfloat16)
        q_lin = proj(5, lo)
        c_lin = proj(1, lo)
        park(slab_f, _gate_preact(xb16, wbd_ref[0, g]))
        park(slab_r, _gate_preact(xb16, wbd_ref[1, g]))
        u_lin = proj(2, lo)
        b_lin = proj(0, lo)
        g_lin = proj(3, lo)
        if g + 1 < n_groups:
            v_next = proj(4, hi)
        sq = _silu(q_lin)
        ya = b_lin * _dwconv_row(c_lin * u_lin, wca_ref[:, lo:hi], CONV_A_LEFT) * _silu(g_lin)
        ya_ref[:, lo:hi] = ya.astype(ya_ref.dtype)
        af, bf = _lru_coeffs(xb, unpark(slab_f, 0), unpark(slab_f, 1),
                             ba_ref[0:1, lo:hi], bx_ref[0:1, lo:hi], c8[0:1, lo:hi])
        hf, pf, hf_end, pf_end = _local_scan(af, bf, False)
        ar, br = _lru_coeffs(xb, unpark(slab_r, 0), unpark(slab_r, 1),
                             ba_ref[1:2, lo:hi], bx_ref[1:2, lo:hi], c8[1:2, lo:hi])
        hr, pr, hr_end, pr_end = _local_scan(ar, br, True)
        s_ref[:, lo:hi] = (hf + hr) * sq
        pf_ref[:, lo:hi] = (pf * sq).astype(pf_ref.dtype)
        pr_ref[:, lo:hi] = (pr * sq).astype(pr_ref.dtype)
        summ_ref[0, 0, :, lo:hi] = hf_end
        summ_ref[0, 1, :, lo:hi] = pf_end
        summ_ref[0, 2, :, lo:hi] = hr_end
        summ_ref[0, 3, :, lo:hi] = pr_end


def _input_projection(x3, g, mod, w_in_bf, w_conv_a, w_conv_b, b_conv_b, wbd, ba, bx, lam):
    n_seg, seg, d = x3.shape
    w = w_conv_a.shape[1]
    ti = GRID_W
    tb = ti * n_seg
    nb = seg // ti
    const = lambda shape: pl.BlockSpec(shape, lambda k: (0,) * len(shape))
    rows = lambda: pl.BlockSpec((tb, w), lambda k: (k, 0))
    f32 = jnp.float32
    return pl.pallas_call(
        _proj_kernel,
        grid=(nb,),
        in_specs=[pl.BlockSpec((n_seg, ti, d), lambda k: (0, k, 0)), const((1, d)), const(mod.shape),
                  pl.BlockSpec(w_in_bf.shape, lambda k: (0, 0), pipeline_mode=pl.Buffered(1)),
                  const(w_conv_a.shape), const(w_conv_b.shape), const((1, w)),
                  const(wbd.shape), const(ba.shape), const(bx.shape), const(lam.shape)],
        out_specs=[rows(), rows(), rows(), rows(),
                   pl.BlockSpec((1, 4, n_seg, w), lambda k: (k, 0, 0, 0))],
        out_shape=[jax.ShapeDtypeStruct((nb * tb, w), jnp.bfloat16),
                   jax.ShapeDtypeStruct((nb * tb, w), f32),
                   jax.ShapeDtypeStruct((nb * tb, w), jnp.bfloat16),
                   jax.ShapeDtypeStruct((nb * tb, w), jnp.bfloat16),
                   jax.ShapeDtypeStruct((nb, 4, n_seg, w), f32)],
        scratch_shapes=[pltpu.VMEM((d // LANES, tb, LANES), f32)],
        compiler_params=pltpu.CompilerParams(dimension_semantics=("arbitrary",),
                                             vmem_limit_bytes=VMEM_LIMIT),
        name="proj",
    )(x3, g, mod, w_in_bf, w_conv_a, w_conv_b, b_conv_b, wbd, ba, bx, lam)


def _carry_kernel(summ_ref, h0_ref, cf_ref, cr_ref):
    nb, _, n_seg, _ = summ_ref.shape
    c = h0_ref[0:1, :]
    for s in range(n_seg):
        for k in range(nb):
            cf_ref[k, s:s + 1, :] = c
            c = summ_ref[k, 0, s:s + 1, :] + summ_ref[k, 1, s:s + 1, :] * c
    c = h0_ref[1:2, :]
    for s in range(n_seg - 1, -1, -1):
        for k in range(nb - 1, -1, -1):
            cr_ref[k, s:s + 1, :] = c
            c = summ_ref[k, 2, s:s + 1, :] + summ_ref[k, 3, s:s + 1, :] * c


def _chunk_carries(summ, h0):
    nb, _, n_seg, w = summ.shape
    full = lambda shape: pl.BlockSpec(shape, lambda i: (0,) * len(shape))
    return pl.pallas_call(
        _carry_kernel,
        grid=(1,),
        in_specs=[full(summ.shape), full(h0.shape)],
        out_specs=[full((nb, n_seg, w)), full((nb, n_seg, w))],
        out_shape=[jax.ShapeDtypeStruct((nb, n_seg, w), jnp.float32)] * 2,
        compiler_params=pltpu.CompilerParams(dimension_semantics=("arbitrary",)),
        name="carry",
    )(summ, h0)


def _out_kernel(x_ref, ya_ref, s_ref, pf_ref, pr_ref, cf_ref, cr_ref, w_ref, mod_ref, g_ref, o_ref, r_s):
    n_seg, ti, d = x_ref.shape
    wdt = ya_ref.shape[1]
    n_slab = d // LANES
    cf = jnp.tile(cf_ref[0], (ti, 1))
    cr = jnp.tile(cr_ref[0], (ti, 1))
    f32 = jnp.float32
    yb = (s_ref[...] + pf_ref[...].astype(f32) * cf + pr_ref[...].astype(f32) * cr).astype(jnp.bfloat16)
    out = jnp.dot(ya_ref[...], w_ref[0:wdt, :], preferred_element_type=jnp.float32)
    out = out + jnp.dot(yb, w_ref[wdt:2 * wdt, :], preferred_element_type=jnp.float32)
    for c in range(n_slab):
        r_s[c] = out[:, c * LANES:(c + 1) * LANES]
    gate = mod_ref[0:1, 2 * d:3 * d]
    for s in range(n_seg):
        out_s = jnp.concatenate([r_s[c, pl.ds(s, ti, stride=n_seg), :] for c in range(n_slab)], axis=1)
        new = x_ref[s] + gate * out_s
        ms = jnp.mean(new * new, axis=-1, keepdims=True)
        o_ref[s] = (new * lax.rsqrt(ms + EPS)) * g_ref[...]


def _output_projection(x3, ya, s, pf, pr, cf, cr, w_out_bf, mod, final_g):
    n_seg, seg, d = x3.shape
    w = ya.shape[1]
    ti = GRID_W
    tb = ti * n_seg
    nb = seg // ti
    const = lambda shape: pl.BlockSpec(shape, lambda k: (0,) * len(shape))
    rows = lambda: pl.BlockSpec((tb, w), lambda k: (k, 0))
    segs = lambda: pl.BlockSpec((n_seg, ti, d), lambda k: (0, k, 0))
    carry = lambda: pl.BlockSpec((1, n_seg, w), lambda k: (k, 0, 0))
    return pl.pallas_call(
        _out_kernel,
        grid=(nb,),
        in_specs=[segs(), rows(), rows(), rows(), rows(), carry(), carry(),
                  pl.BlockSpec(w_out_bf.shape, lambda k: (0, 0), pipeline_mode=pl.Buffered(1)),
                  const(mod.shape), const((1, d))],
        out_specs=segs(),
        out_shape=jax.ShapeDtypeStruct((n_seg, seg, d), jnp.float32),
        scratch_shapes=[pltpu.VMEM((d // LANES, tb, LANES), jnp.float32)],
        compiler_params=pltpu.CompilerParams(dimension_semantics=("arbitrary",),
                                             vmem_limit_bytes=VMEM_LIMIT),
        name="out",
    )(x3, ya, s, pf, pr, cf, cr, w_out_bf, mod, final_g)


def _block_diag_groups(w):
    n_dir, n_heads, dh, _ = w.shape
    per = MXU_TILE // dh
    w5 = w.reshape(n_dir, n_heads // per, per, dh, dh)
    eye = jnp.eye(per, dtype=w.dtype)
    bd = w5[:, :, :, :, None, :] * eye[None, None, :, None, :, None]
    return bd.reshape(n_dir, n_heads // per, MXU_TILE, MXU_TILE)


def kernel(x, c, ctx, c_ctx, norm_g, w_ada, b_ada, w_in, w_conv_a, w_conv_b, b_conv_b,
           lru_wa, lru_ba, lru_wx, lru_bx, lru_lambda, w_out, final_g):
    assert w_in.shape[0] == 1 and x.shape[0] == 1, "single layer, single batch element"
    _, l, d = x.shape
    assert l % (N_SEG * GRID_W) == 0

    cc = jnp.concatenate([c, c_ctx[None, :], jnp.zeros((SUBLANES - 2, d), x.dtype)], axis=0)
    mod = _modulation(cc, w_ada[0], b_ada)
    g = norm_g

    w_in_bf = w_in[0].astype(jnp.bfloat16)
    w_out_bf = w_out[0].astype(jnp.bfloat16)
    wbd = jnp.concatenate([_block_diag_groups(lru_wa[0]), _block_diag_groups(lru_wx[0])],
                          axis=-1).astype(jnp.bfloat16)
    ba, bx, lam = lru_ba[0], lru_bx[0], lru_lambda[0]

    x3 = x[0].reshape(N_SEG, l // N_SEG, d)
    h0 = _context_state(ctx[0], g, mod, w_in_bf, w_conv_b[0], b_conv_b, wbd, ba, bx, lam)
    ya, s, pf, pr, summ = _input_projection(x3, g, mod, w_in_bf, w_conv_a[0], w_conv_b[0], b_conv_b,
                                            wbd, ba, bx, lam)
    cf, cr = _chunk_carries(summ, h0)
    out = _output_projection(x3, ya, s, pf, pr, cf, cr, w_out_bf, mod, final_g[None, :])
    return out.reshape(1, l, d)
```

```python
import jax
import jax.numpy as jnp
from jax import lax
from jax.experimental import pallas as pl
from jax.experimental.pallas import tpu as pltpu

EPS = 1e-6
LRU_C = 8.0
GRID_W = 64
CONV_A_LEFT = 1
CONV_B_LEFT = 2
SUBLANES = 8
LANES = 128
MXU_TILE = 256
N_SEG = SUBLANES
VMEM_LIMIT = 60 * 1024 * 1024


def _sigmoid(z):
    return 0.5 * jnp.tanh(0.5 * z) + 0.5


def _silu(z):
    return z * _sigmoid(z)


def _rms_mod(x, g_ref, mod_ref, row):
    d = x.shape[-1]
    shift = mod_ref[row:row + 1, 0:d]
    scale = g_ref[...] * (1.0 + mod_ref[row:row + 1, d:2 * d])
    ms = jnp.mean(x * x, axis=-1, keepdims=True)
    return (x * lax.rsqrt(ms + EPS)) * scale + shift


def _c_softplus_neg(lam):
    z = -lam
    return LRU_C * (jnp.maximum(z, 0.0) + jnp.log1p(jnp.exp(-jnp.abs(z))))


def _gate_preact(xg, wbd):
    return jnp.dot(xg.astype(jnp.bfloat16), wbd, preferred_element_type=jnp.float32)


def _lru_coeffs(xg, pre_r, pre_i, ba, bx, c8):
    r = _sigmoid(pre_r + ba)
    i = _sigmoid(pre_i + bx)
    u = r * c8
    a = jnp.exp(-u)
    one_minus_a2 = jnp.tanh(u) * (1.0 + a * a)
    return a, jnp.sqrt(one_minus_a2) * (i * xg)


def _mod_kernel(c_ref, cx_ref, w_ref, b_ref, wa_ref, wx_ref, o_ref, wbd_ref):
    rows, d = o_ref.shape[0], c_ref.shape[1]
    row = lax.broadcasted_iota(jnp.int32, (rows, d), 0)
    cc = jnp.where(row == 0, jnp.broadcast_to(c_ref[...], (rows, d)),
                   jnp.where(row == 1, jnp.broadcast_to(cx_ref[...], (rows, d)), 0.0))
    o_ref[...] = jnp.dot(_silu(cc), w_ref[...], preferred_element_type=jnp.float32) + b_ref[...]

    @pl.when(pl.program_id(0) == 0)
    def _():
        n_dir, n_heads, dh, _ = wa_ref.shape
        per = MXU_TILE // dh
        wbd_ref[...] = jnp.zeros(wbd_ref.shape, wbd_ref.dtype)
        for dr in range(n_dir):
            for h in range(n_heads):
                g, r0 = h // per, (h % per) * dh
                wbd_ref[dr, g, r0:r0 + dh, r0:r0 + dh] = wa_ref[dr, h].astype(wbd_ref.dtype)
                wbd_ref[dr, g, r0:r0 + dh, MXU_TILE + r0:MXU_TILE + r0 + dh] = wx_ref[dr, h].astype(wbd_ref.dtype)


def _modulation(c, c_ctx, w_ada, b_ada, wa, wx):
    d = c.shape[1]
    n = w_ada.shape[1]
    n_dir, n_heads, dh, _ = wa.shape
    tn = 1024
    wbd_shape = (n_dir, n_heads * dh // MXU_TILE, MXU_TILE, 2 * MXU_TILE)
    const = lambda shape: pl.BlockSpec(shape, lambda j: (0,) * len(shape))
    return pl.pallas_call(
        _mod_kernel,
        grid=(n // tn,),
        in_specs=[const((1, d)), const((1, d)),
                  pl.BlockSpec((d, tn), lambda j: (0, j)),
                  pl.BlockSpec((1, tn), lambda j: (0, j)),
                  const(wa.shape), const(wx.shape)],
        out_specs=[pl.BlockSpec((SUBLANES, tn), lambda j: (0, j)), const(wbd_shape)],
        out_shape=[jax.ShapeDtypeStruct((SUBLANES, n), jnp.float32),
                   jax.ShapeDtypeStruct(wbd_shape, jnp.bfloat16)],
        compiler_params=pltpu.CompilerParams(dimension_semantics=("arbitrary",),
                                             vmem_limit_bytes=VMEM_LIMIT),
        name="mod",
    )(c, c_ctx, w_ada, b_ada, wa, wx)


def _dwconv_seq(x, w, left):
    t = x.shape[0]
    pos = lax.broadcasted_iota(jnp.int32, x.shape, 0)
    out = None
    for j in range(w.shape[0]):
        off = j - left
        if off == 0:
            term = x * w[j:j + 1, :]
        else:
            shifted = pltpu.roll(x, (-off) % t, 0)
            valid = (pos + off >= 0) & (pos + off < t)
            term = jnp.where(valid, shifted, 0.0) * w[j:j + 1, :]
        out = term if out is None else out + term
    return out


def _tile_scan(a, b, reverse):
    sub = lax.broadcasted_iota(jnp.int32, a.shape, 0)
    for d in (1, 2, 4):
        if reverse:
            keep = sub < SUBLANES - d
            shift = SUBLANES - d
        else:
            keep = sub >= d
            shift = d
        a_sh = jnp.where(keep, pltpu.roll(a, shift, 0), 1.0)
        b_sh = jnp.where(keep, pltpu.roll(b, shift, 0), 0.0)
        b = a * b_sh + b
        a = a * a_sh
    return a, b


def _final_states(a_ref, b_ref):
    t = a_ref.shape[1]
    n_tiles = t // SUBLANES
    w = a_ref.shape[2]

    def body(j, carry):
        hf, hr = carry
        rf = pl.multiple_of(j * SUBLANES, SUBLANES)
        rr = pl.multiple_of((n_tiles - 1 - j) * SUBLANES, SUBLANES)
        af, bf = _tile_scan(a_ref[0, pl.ds(rf, SUBLANES), :], b_ref[0, pl.ds(rf, SUBLANES), :], False)
        ar, br = _tile_scan(a_ref[1, pl.ds(rr, SUBLANES), :], b_ref[1, pl.ds(rr, SUBLANES), :], True)
        yf = bf + af * hf
        yr = br + ar * hr
        hf = jnp.broadcast_to(yf[SUBLANES - 1:SUBLANES, :], (SUBLANES, w))
        hr = jnp.broadcast_to(yr[0:1, :], (SUBLANES, w))
        return hf, hr

    zero = jnp.zeros((SUBLANES, w), jnp.float32)
    return lax.fori_loop(0, n_tiles, body, (zero, zero), unroll=2)


def _ctx_kernel(x_ref, g_ref, mod_ref, wv_ref, wcb_ref, bcb_ref, wbd_ref, ba_ref, bx_ref, lam_ref,
                h0_ref, a_s, b_s):
    w = wv_ref.shape[1]
    hc = _rms_mod(x_ref[...], g_ref, mod_ref, 1)
    vc = jnp.dot(hc.astype(jnp.bfloat16), wv_ref[...], preferred_element_type=jnp.float32)
    xb = _dwconv_seq(vc, wcb_ref[...], CONV_B_LEFT) + bcb_ref[...]
    c8 = _c_softplus_neg(lam_ref[...])
    for d in range(2):
        for g in range(w // MXU_TILE):
            lo, hi = g * MXU_TILE, (g + 1) * MXU_TILE
            pre = _gate_preact(xb[:, lo:hi], wbd_ref[d, g])
            a, b = _lru_coeffs(xb[:, lo:hi], pre[:, :MXU_TILE], pre[:, MXU_TILE:],
                               ba_ref[d:d + 1, lo:hi], bx_ref[d:d + 1, lo:hi], c8[d:d + 1, lo:hi])
            a_s[d, :, lo:hi] = a
            b_s[d, :, lo:hi] = b
    hf, hr = _final_states(a_s, b_s)
    h0_ref[0:1, :] = hf[0:1, :]
    h0_ref[1:2, :] = hr[0:1, :]


def _context_state(ctx, g, mod, w_in_bf, w_conv_b, b_conv_b, wbd, ba, bx, lam):
    t, d = ctx.shape
    w = w_conv_b.shape[1]
    v_block = 4
    full = lambda shape: pl.BlockSpec(shape, lambda i: (0,) * len(shape))
    return pl.pallas_call(
        _ctx_kernel,
        grid=(1,),
        in_specs=[full((t, d)), full((1, d)), full(mod.shape),
                  pl.BlockSpec((d, w), lambda i: (0, v_block)),
                  full(w_conv_b.shape), full((1, w)), full(wbd.shape), full(ba.shape), full(bx.shape),
                  full(lam.shape)],
        out_specs=full((2, w)),
        out_shape=jax.ShapeDtypeStruct((2, w), jnp.float32),
        scratch_shapes=[pltpu.VMEM((2, t, w), jnp.float32), pltpu.VMEM((2, t, w), jnp.float32)],
        compiler_params=pltpu.CompilerParams(dimension_semantics=("arbitrary",),
                                             vmem_limit_bytes=VMEM_LIMIT),
        name="ctx",
    )(ctx, g, mod, w_in_bf, w_conv_b, b_conv_b, wbd, ba, bx, lam)


def _shift_tokens(x, off):
    if off == 0:
        return x
    pad = jnp.zeros((abs(off) * N_SEG, x.shape[1]), x.dtype)
    if off < 0:
        return jnp.concatenate([pad, x[:off * N_SEG]], axis=0)
    return jnp.concatenate([x[off * N_SEG:], pad], axis=0)


def _dwconv_row(x, w, left):
    out = None
    for j in range(w.shape[0]):
        term = _shift_tokens(x, j - left) * w[j:j + 1, :]
        out = term if out is None else out + term
    return out


def _local_scan(a, b, reverse):
    n = a.shape[0] // N_SEG
    order = range(n - 1, -1, -1) if reverse else range(n)
    hs, ps = [None] * n, [None] * n
    h = p = None
    for t in order:
        at, bt = a[t * N_SEG:(t + 1) * N_SEG], b[t * N_SEG:(t + 1) * N_SEG]
        h = bt if h is None else at * h + bt
        p = at if p is None else at * p
        hs[t], ps[t] = h, p
    return jnp.concatenate(hs, axis=0), jnp.concatenate(ps, axis=0), h, p


def _proj_kernel(x_ref, g_ref, mod_ref, w_ref, wca_ref, wcb_ref, bcb_ref, wbd_ref, ba_ref, bx_ref, lam_ref,
                 ya_ref, s_ref, pf_ref, pr_ref, summ_ref, hi_s):
    n_seg, ti, d = x_ref.shape
    wdt = ya_ref.shape[1]
    n_slab = d // LANES

    for s in range(n_seg):
        hs = _rms_mod(x_ref[s], g_ref, mod_ref, 0)
        for c in range(n_slab):
            hi_s[c, pl.ds(s, ti, stride=n_seg), :] = hs[:, c * LANES:(c + 1) * LANES]
    hl = jnp.concatenate([hi_s[c] for c in range(n_slab)], axis=1).astype(jnp.bfloat16)

    def proj(group, lo):
        c0 = group * wdt + lo
        return jnp.dot(hl, w_ref[:, c0:c0 + MXU_TILE], preferred_element_type=jnp.float32)

    gate_slabs = 2 * MXU_TILE // LANES
    assert hi_s.shape[0] == n_slab >= 4 * gate_slabs

    def park(slab0, pre):
        for j in range(gate_slabs):
            hi_s[slab0 + j] = pre[:, j * LANES:(j + 1) * LANES]

    def unpark(slab0, half):
        j0 = slab0 + half * (gate_slabs // 2)
        return jnp.concatenate([hi_s[j0 + j] for j in range(gate_slabs // 2)], axis=1)

    c8 = _c_softplus_neg(lam_ref[...])
    n_groups = wdt // MXU_TILE
    v_next = proj(4, 0)
    for g in range(n_groups):
        lo, hi = g * MXU_TILE, (g + 1) * MXU_TILE
        slab_f = (g % 2) * 2 * gate_slabs
        slab_r = slab_f + gate_slabs
        xb = _dwconv_row(v_next, wcb_ref[:, lo:hi], CONV_B_LEFT) + bcb_ref[:, lo:hi]
        xb16 = xb.astype(jnp.bfloat16)
        q_lin = proj(5, lo)
        c_lin = proj(1, lo)
        park(slab_f, _gate_preact(xb16, wbd_ref[0, g]))
        park(slab_r, _gate_preact(xb16, wbd_ref[1, g]))
        u_lin = proj(2, lo)
        b_lin = proj(0, lo)
        g_lin = proj(3, lo)
        if g + 1 < n_groups:
            v_next = proj(4, hi)
        sq = _silu(q_lin)
        ya = b_lin * _dwconv_row(c_lin * u_lin, wca_ref[:, lo:hi], CONV_A_LEFT) * _silu(g_lin)
        ya_ref[:, lo:hi] = ya.astype(ya_ref.dtype)
        af, bf = _lru_coeffs(xb, unpark(slab_f, 0), unpark(slab_f, 1),
                             ba_ref[0:1, lo:hi], bx_ref[0:1, lo:hi], c8[0:1, lo:hi])
        hf, pf, hf_end, pf_end = _local_scan(af, bf, False)
        ar, br = _lru_coeffs(xb, unpark(slab_r, 0), unpark(slab_r, 1),
                             ba_ref[1:2, lo:hi], bx_ref[1:2, lo:hi], c8[1:2, lo:hi])
        hr, pr, hr_end, pr_end = _local_scan(ar, br, True)
        s_ref[:, lo:hi] = ((hf + hr) * sq).astype(s_ref.dtype)
        pf_ref[:, lo:hi] = (pf * sq).astype(pf_ref.dtype)
        pr_ref[:, lo:hi] = (pr * sq).astype(pr_ref.dtype)
        summ_ref[0, 0, :, lo:hi] = hf_end
        summ_ref[0, 1, :, lo:hi] = pf_end
        summ_ref[0, 2, :, lo:hi] = hr_end
        summ_ref[0, 3, :, lo:hi] = pr_end


def _input_projection(x3, g, mod, w_in_bf, w_conv_a, w_conv_b, b_conv_b, wbd, ba, bx, lam):
    n_seg, seg, d = x3.shape
    w = w_conv_a.shape[1]
    ti = GRID_W
    tb = ti * n_seg
    nb = seg // ti
    const = lambda shape: pl.BlockSpec(shape, lambda k: (0,) * len(shape))
    rows = lambda: pl.BlockSpec((tb, w), lambda k: (k, 0))
    f32 = jnp.float32
    return pl.pallas_call(
        _proj_kernel,
        grid=(nb,),
        in_specs=[pl.BlockSpec((n_seg, ti, d), lambda k: (0, k, 0)), const((1, d)), const(mod.shape),
                  pl.BlockSpec(w_in_bf.shape, lambda k: (0, 0), pipeline_mode=pl.Buffered(1)),
                  const(w_conv_a.shape), const(w_conv_b.shape), const((1, w)),
                  const(wbd.shape), const(ba.shape), const(bx.shape), const(lam.shape)],
        out_specs=[rows(), rows(), rows(), rows(),
                   pl.BlockSpec((1, 4, n_seg, w), lambda k: (k, 0, 0, 0))],
        out_shape=[jax.ShapeDtypeStruct((nb * tb, w), jnp.bfloat16),
                   jax.ShapeDtypeStruct((nb * tb, w), jnp.bfloat16),
                   jax.ShapeDtypeStruct((nb * tb, w), jnp.bfloat16),
                   jax.ShapeDtypeStruct((nb * tb, w), jnp.bfloat16),
                   jax.ShapeDtypeStruct((nb, 4, n_seg, w), f32)],
        scratch_shapes=[pltpu.VMEM((d // LANES, tb, LANES), f32)],
        compiler_params=pltpu.CompilerParams(dimension_semantics=("arbitrary",),
                                             vmem_limit_bytes=VMEM_LIMIT),
        name="proj",
    )(x3, g, mod, w_in_bf, w_conv_a, w_conv_b, b_conv_b, wbd, ba, bx, lam)


def _carry_kernel(summ_ref, h0_ref, cf_ref, cr_ref):
    nb, _, n_seg, _ = summ_ref.shape
    c = h0_ref[0:1, :]
    for s in range(n_seg):
        for k in range(nb):
            cf_ref[k, s:s + 1, :] = c
            c = summ_ref[k, 0, s:s + 1, :] + summ_ref[k, 1, s:s + 1, :] * c
    c = h0_ref[1:2, :]
    for s in range(n_seg - 1, -1, -1):
        for k in range(nb - 1, -1, -1):
            cr_ref[k, s:s + 1, :] = c
            c = summ_ref[k, 2, s:s + 1, :] + summ_ref[k, 3, s:s + 1, :] * c


def _chunk_carries(summ, h0):
    nb, _, n_seg, w = summ.shape
    full = lambda shape: pl.BlockSpec(shape, lambda i: (0,) * len(shape))
    return pl.pallas_call(
        _carry_kernel,
        grid=(1,),
        in_specs=[full(summ.shape), full(h0.shape)],
        out_specs=[full((nb, n_seg, w)), full((nb, n_seg, w))],
        out_shape=[jax.ShapeDtypeStruct((nb, n_seg, w), jnp.float32)] * 2,
        compiler_params=pltpu.CompilerParams(dimension_semantics=("arbitrary",)),
        name="carry",
    )(summ, h0)


def _out_kernel(x_ref, ya_ref, s_ref, pf_ref, pr_ref, cf_ref, cr_ref, wf_ref, mod_ref, g_ref, o_ref, r_s, w_ref):
    n_seg, ti, d = x_ref.shape
    wdt = ya_ref.shape[1]
    n_slab = d // LANES

    @pl.when(pl.program_id(0) == 0)
    def _():
        rows = MXU_TILE

        def cast_rows(i, carry):
            r0 = pl.multiple_of(i * rows, rows)
            w_ref[pl.ds(r0, rows), :] = wf_ref[pl.ds(r0, rows), :].astype(w_ref.dtype)
            return carry

        lax.fori_loop(0, wf_ref.shape[0] // rows, cast_rows, 0)

    cf = jnp.tile(cf_ref[0], (ti, 1))
    cr = jnp.tile(cr_ref[0], (ti, 1))
    f32 = jnp.float32
    yb = (s_ref[...].astype(f32) + pf_ref[...].astype(f32) * cf
          + pr_ref[...].astype(f32) * cr).astype(jnp.bfloat16)
    out = jnp.dot(ya_ref[...], w_ref[0:wdt, :], preferred_element_type=jnp.float32)
    out = out + jnp.dot(yb, w_ref[wdt:2 * wdt, :], preferred_element_type=jnp.float32)
    for c in range(n_slab):
        r_s[c] = out[:, c * LANES:(c + 1) * LANES]
    gate = mod_ref[0:1, 2 * d:3 * d]
    for s in range(n_seg):
        out_s = jnp.concatenate([r_s[c, pl.ds(s, ti, stride=n_seg), :] for c in range(n_slab)], axis=1)
        new = x_ref[s] + gate * out_s
        ms = jnp.mean(new * new, axis=-1, keepdims=True)
        o_ref[s] = (new * lax.rsqrt(ms + EPS)) * g_ref[...]


def _output_projection(x3, ya, s, pf, pr, cf, cr, w_out, mod, final_g):
    n_seg, seg, d = x3.shape
    w = ya.shape[1]
    ti = GRID_W
    tb = ti * n_seg
    nb = seg // ti
    const = lambda shape: pl.BlockSpec(shape, lambda k: (0,) * len(shape))
    rows = lambda: pl.BlockSpec((tb, w), lambda k: (k, 0))
    segs = lambda: pl.BlockSpec((n_seg, ti, d), lambda k: (0, k, 0))
    carry = lambda: pl.BlockSpec((1, n_seg, w), lambda k: (k, 0, 0))
    return pl.pallas_call(
        _out_kernel,
        grid=(nb,),
        in_specs=[segs(), rows(), rows(), rows(), rows(), carry(), carry(),
                  pl.BlockSpec(w_out.shape, lambda k: (0, 0), pipeline_mode=pl.Buffered(1)),
                  const(mod.shape), const((1, d))],
        out_specs=segs(),
        out_shape=jax.ShapeDtypeStruct((n_seg, seg, d), jnp.float32),
        scratch_shapes=[pltpu.VMEM((d // LANES, tb, LANES), jnp.float32),
                        pltpu.VMEM(w_out.shape, jnp.bfloat16)],
        compiler_params=pltpu.CompilerParams(dimension_semantics=("arbitrary",),
                                             vmem_limit_bytes=VMEM_LIMIT),
        name="out",
    )(x3, ya, s, pf, pr, cf, cr, w_out, mod, final_g)


def kernel(x, c, ctx, c_ctx, norm_g, w_ada, b_ada, w_in, w_conv_a, w_conv_b, b_conv_b,
           lru_wa, lru_ba, lru_wx, lru_bx, lru_lambda, w_out, final_g):
    assert w_in.shape[0] == 1 and x.shape[0] == 1, "single layer, single batch element"
    _, l, d = x.shape
    assert l % (N_SEG * GRID_W) == 0

    mod, wbd = _modulation(c, c_ctx[None, :], w_ada[0], b_ada, lru_wa[0], lru_wx[0])
    g = norm_g

    w_in_bf = w_in[0].astype(jnp.bfloat16)
    ba, bx, lam = lru_ba[0], lru_bx[0], lru_lambda[0]

    x3 = x[0].reshape(N_SEG, l // N_SEG, d)
    h0 = _context_state(ctx[0], g, mod, w_in_bf, w_conv_b[0], b_conv_b, wbd, ba, bx, lam)
    ya, s, pf, pr, summ = _input_projection(x3, g, mod, w_in_bf, w_conv_a[0], w_conv_b[0], b_conv_b,
                                            wbd, ba, bx, lam)
    cf, cr = _chunk_carries(summ, h0)
    out = _output_projection(x3, ya, s, pf, pr, cf, cr, w_out[0], mod, final_g[None, :])
    return out.reshape(1, l, d)
```

```python
import jax
import jax.numpy as jnp
from jax import lax
from jax.experimental import pallas as pl
from jax.experimental.pallas import tpu as pltpu

EPS = 1e-6
LRU_C = 8.0
GRID_W = 64
CONV_A_LEFT = 1
CONV_B_LEFT = 2
SUBLANES = 8
LANES = 128
MXU_TILE = 256
N_SEG = SUBLANES
VMEM_LIMIT = 60 * 1024 * 1024


def _sigmoid(z):
    return 0.5 * jnp.tanh(0.5 * z) + 0.5


def _silu(z):
    return z * _sigmoid(z)


def _rms_mod(x, g_ref, mod_ref, row):
    d = x.shape[-1]
    shift = mod_ref[row:row + 1, 0:d]
    scale = g_ref[...] * (1.0 + mod_ref[row:row + 1, d:2 * d])
    ms = jnp.mean(x * x, axis=-1, keepdims=True)
    return (x * lax.rsqrt(ms + EPS)) * scale + shift


def _c_softplus_neg(lam):
    z = -lam
    return LRU_C * (jnp.maximum(z, 0.0) + jnp.log1p(jnp.exp(-jnp.abs(z))))


def _gate_preact(xg, wbd):
    return jnp.dot(xg.astype(jnp.bfloat16), wbd, preferred_element_type=jnp.float32)


def _lru_coeffs(xg, pre_r, pre_i, ba, bx, c8):
    r = _sigmoid(pre_r + ba)
    i = _sigmoid(pre_i + bx)
    u = r * c8
    a = jnp.exp(-u)
    one_minus_a2 = jnp.tanh(u) * (1.0 + a * a)
    return a, jnp.sqrt(one_minus_a2) * (i * xg)


def _mod_kernel(c_ref, cx_ref, w_ref, b_ref, wa_ref, wx_ref, o_ref, wbd_ref):
    rows, d = o_ref.shape[0], c_ref.shape[1]
    row = lax.broadcasted_iota(jnp.int32, (rows, d), 0)
    cc = jnp.where(row == 0, jnp.broadcast_to(c_ref[...], (rows, d)),
                   jnp.where(row == 1, jnp.broadcast_to(cx_ref[...], (rows, d)), 0.0))
    o_ref[...] = jnp.dot(_silu(cc), w_ref[...], preferred_element_type=jnp.float32) + b_ref[...]

    @pl.when(pl.program_id(0) == 0)
    def _():
        n_dir, n_heads, dh, _ = wa_ref.shape
        per = MXU_TILE // dh
        wbd_ref[...] = jnp.zeros(wbd_ref.shape, wbd_ref.dtype)
        for dr in range(n_dir):
            for h in range(n_heads):
                g, r0 = h // per, (h % per) * dh
                wbd_ref[dr, g, r0:r0 + dh, r0:r0 + dh] = wa_ref[dr, h].astype(wbd_ref.dtype)
                wbd_ref[dr, g, r0:r0 + dh, MXU_TILE + r0:MXU_TILE + r0 + dh] = wx_ref[dr, h].astype(wbd_ref.dtype)


def _modulation(c, c_ctx, w_ada, b_ada, wa, wx):
    d = c.shape[1]
    n = w_ada.shape[1]
    n_dir, n_heads, dh, _ = wa.shape
    tn = 1024
    wbd_shape = (n_dir, n_heads * dh // MXU_TILE, MXU_TILE, 2 * MXU_TILE)
    const = lambda shape: pl.BlockSpec(shape, lambda j: (0,) * len(shape))
    return pl.pallas_call(
        _mod_kernel,
        grid=(n // tn,),
        in_specs=[const((1, d)), const((1, d)),
                  pl.BlockSpec((d, tn), lambda j: (0, j)),
                  pl.BlockSpec((1, tn), lambda j: (0, j)),
                  const(wa.shape), const(wx.shape)],
        out_specs=[pl.BlockSpec((SUBLANES, tn), lambda j: (0, j)), const(wbd_shape)],
        out_shape=[jax.ShapeDtypeStruct((SUBLANES, n), jnp.float32),
                   jax.ShapeDtypeStruct(wbd_shape, jnp.bfloat16)],
        compiler_params=pltpu.CompilerParams(dimension_semantics=("arbitrary",),
                                             vmem_limit_bytes=VMEM_LIMIT),
        name="mod",
    )(c, c_ctx, w_ada, b_ada, wa, wx)


def _dwconv_seq(x, w, left):
    t = x.shape[0]
    pos = lax.broadcasted_iota(jnp.int32, x.shape, 0)
    out = None
    for j in range(w.shape[0]):
        off = j - left
        if off == 0:
            term = x * w[j:j + 1, :]
        else:
            shifted = pltpu.roll(x, (-off) % t, 0)
            valid = (pos + off >= 0) & (pos + off < t)
            term = jnp.where(valid, shifted, 0.0) * w[j:j + 1, :]
        out = term if out is None else out + term
    return out


def _tile_scan(a, b, reverse):
    sub = lax.broadcasted_iota(jnp.int32, a.shape, 0)
    for d in (1, 2, 4):
        if reverse:
            keep = sub < SUBLANES - d
            shift = SUBLANES - d
        else:
            keep = sub >= d
            shift = d
        a_sh = jnp.where(keep, pltpu.roll(a, shift, 0), 1.0)
        b_sh = jnp.where(keep, pltpu.roll(b, shift, 0), 0.0)
        b = a * b_sh + b
        a = a * a_sh
    return a, b


def _final_states(a_ref, b_ref):
    t = a_ref.shape[1]
    n_tiles = t // SUBLANES
    w = a_ref.shape[2]

    def body(j, carry):
        hf, hr = carry
        rf = pl.multiple_of(j * SUBLANES, SUBLANES)
        rr = pl.multiple_of((n_tiles - 1 - j) * SUBLANES, SUBLANES)
        af, bf = _tile_scan(a_ref[0, pl.ds(rf, SUBLANES), :], b_ref[0, pl.ds(rf, SUBLANES), :], False)
        ar, br = _tile_scan(a_ref[1, pl.ds(rr, SUBLANES), :], b_ref[1, pl.ds(rr, SUBLANES), :], True)
        yf = bf + af * hf
        yr = br + ar * hr
        hf = jnp.broadcast_to(yf[SUBLANES - 1:SUBLANES, :], (SUBLANES, w))
        hr = jnp.broadcast_to(yr[0:1, :], (SUBLANES, w))
        return hf, hr

    zero = jnp.zeros((SUBLANES, w), jnp.float32)
    return lax.fori_loop(0, n_tiles, body, (zero, zero), unroll=2)


def _ctx_kernel(x_ref, g_ref, mod_ref, wv_ref, wcb_ref, bcb_ref, wbd_ref, ba_ref, bx_ref, lam_ref,
                h0_ref, a_s, b_s):
    w = wv_ref.shape[1]
    hc = _rms_mod(x_ref[...], g_ref, mod_ref, 1)
    vc = jnp.dot(hc.astype(jnp.bfloat16), wv_ref[...].astype(jnp.bfloat16), preferred_element_type=jnp.float32)
    xb = _dwconv_seq(vc, wcb_ref[...], CONV_B_LEFT) + bcb_ref[...]
    c8 = _c_softplus_neg(lam_ref[...])
    for d in range(2):
        for g in range(w // MXU_TILE):
            lo, hi = g * MXU_TILE, (g + 1) * MXU_TILE
            pre = _gate_preact(xb[:, lo:hi], wbd_ref[d, g])
            a, b = _lru_coeffs(xb[:, lo:hi], pre[:, :MXU_TILE], pre[:, MXU_TILE:],
                               ba_ref[d:d + 1, lo:hi], bx_ref[d:d + 1, lo:hi], c8[d:d + 1, lo:hi])
            a_s[d, :, lo:hi] = a
            b_s[d, :, lo:hi] = b
    hf, hr = _final_states(a_s, b_s)
    h0_ref[0:1, :] = hf[0:1, :]
    h0_ref[1:2, :] = hr[0:1, :]


def _context_state(ctx, g, mod, w_in, w_conv_b, b_conv_b, wbd, ba, bx, lam):
    t, d = ctx.shape
    w = w_conv_b.shape[1]
    v_block = 4
    full = lambda shape: pl.BlockSpec(shape, lambda i: (0,) * len(shape))
    return pl.pallas_call(
        _ctx_kernel,
        grid=(1,),
        in_specs=[full((t, d)), full((1, d)), full(mod.shape),
                  pl.BlockSpec((d, w), lambda i: (0, v_block)),
                  full(w_conv_b.shape), full((1, w)), full(wbd.shape), full(ba.shape), full(bx.shape),
                  full(lam.shape)],
        out_specs=full((2, w)),
        out_shape=jax.ShapeDtypeStruct((2, w), jnp.float32),
        scratch_shapes=[pltpu.VMEM((2, t, w), jnp.float32), pltpu.VMEM((2, t, w), jnp.float32)],
        compiler_params=pltpu.CompilerParams(dimension_semantics=("arbitrary",),
                                             vmem_limit_bytes=VMEM_LIMIT),
        name="ctx",
    )(ctx, g, mod, w_in, w_conv_b, b_conv_b, wbd, ba, bx, lam)


def _shift_tokens(x, off):
    if off == 0:
        return x
    pad = jnp.zeros((abs(off) * N_SEG, x.shape[1]), x.dtype)
    if off < 0:
        return jnp.concatenate([pad, x[:off * N_SEG]], axis=0)
    return jnp.concatenate([x[off * N_SEG:], pad], axis=0)


def _dwconv_row(x, w, left):
    out = None
    for j in range(w.shape[0]):
        term = _shift_tokens(x, j - left) * w[j:j + 1, :]
        out = term if out is None else out + term
    return out


def _local_scan(a, b, reverse):
    n = a.shape[0] // N_SEG
    order = range(n - 1, -1, -1) if reverse else range(n)
    hs, ps = [None] * n, [None] * n
    h = p = None
    for t in order:
        at, bt = a[t * N_SEG:(t + 1) * N_SEG], b[t * N_SEG:(t + 1) * N_SEG]
        h = bt if h is None else at * h + bt
        p = at if p is None else at * p
        hs[t], ps[t] = h, p
    return jnp.concatenate(hs, axis=0), jnp.concatenate(ps, axis=0), h, p


def _proj_kernel(x_ref, g_ref, mod_ref, w_ref, wca_ref, wcb_ref, bcb_ref, wbd_ref, ba_ref, bx_ref, lam_ref,
                 ya_ref, s_ref, pf_ref, pr_ref, summ_ref, hi_s):
    n_seg, ti, d = x_ref.shape
    wdt = ya_ref.shape[1]
    n_slab = d // LANES

    for s in range(n_seg):
        hs = _rms_mod(x_ref[s], g_ref, mod_ref, 0)
        for c in range(n_slab):
            hi_s[c, pl.ds(s, ti, stride=n_seg), :] = hs[:, c * LANES:(c + 1) * LANES]
    hl = jnp.concatenate([hi_s[c] for c in range(n_slab)], axis=1).astype(jnp.bfloat16)

    def product(c0, n_cols):
        return jnp.dot(hl, w_ref[:, c0:c0 + n_cols], preferred_element_type=jnp.float32)

    gate_slabs = 2 * MXU_TILE // LANES
    assert n_slab >= 4 * gate_slabs

    def park(slab0, xg, wbd):
        half = gate_slabs // 2
        for part in range(2):
            pre = _gate_preact(xg, wbd[:, part * MXU_TILE:(part + 1) * MXU_TILE])
            for j in range(half):
                hi_s[slab0 + part * half + j] = pre[:, j * LANES:(j + 1) * LANES]

    def unpark(slab0, half):
        j0 = slab0 + half * (gate_slabs // 2)
        return jnp.concatenate([hi_s[j0 + j] for j in range(gate_slabs // 2)], axis=1)

    c8 = _c_softplus_neg(lam_ref[...])
    n_groups = wdt // MXU_TILE
    v_next = product(0, MXU_TILE)
    c0 = MXU_TILE
    for g in range(n_groups):
        lo, hi = g * MXU_TILE, (g + 1) * MXU_TILE
        slab_f = (g % 2) * 2 * gate_slabs
        slab_r = slab_f + gate_slabs
        xb = _dwconv_row(v_next, wcb_ref[:, lo:hi], CONV_B_LEFT) + bcb_ref[:, lo:hi]
        xb16 = xb.astype(jnp.bfloat16)
        park(slab_f, xb16, wbd_ref.at[0, g])
        park(slab_r, xb16, wbd_ref.at[1, g])
        n_parts = 6 if g + 1 < n_groups else 5
        wide = product(c0, n_parts * MXU_TILE)
        c0 += n_parts * MXU_TILE
        parts = [wide[:, j * MXU_TILE:(j + 1) * MXU_TILE] for j in range(n_parts)]
        if g + 1 < n_groups:
            v_next = parts.pop(0)
        q_lin, c_lin, u_lin, b_lin, g_lin = parts
        sq = _silu(q_lin)
        ya = b_lin * _dwconv_row(c_lin * u_lin, wca_ref[:, lo:hi], CONV_A_LEFT) * _silu(g_lin)
        ya_ref[:, lo:hi] = ya.astype(ya_ref.dtype)
        af, bf = _lru_coeffs(xb, unpark(slab_f, 0), unpark(slab_f, 1),
                             ba_ref[0:1, lo:hi], bx_ref[0:1, lo:hi], c8[0:1, lo:hi])
        hf, pf, hf_end, pf_end = _local_scan(af, bf, False)
        ar, br = _lru_coeffs(xb, unpark(slab_r, 0), unpark(slab_r, 1),
                             ba_ref[1:2, lo:hi], bx_ref[1:2, lo:hi], c8[1:2, lo:hi])
        hr, pr, hr_end, pr_end = _local_scan(ar, br, True)
        s_ref[:, lo:hi] = ((hf + hr) * sq).astype(s_ref.dtype)
        pf_ref[:, lo:hi] = (pf * sq).astype(pf_ref.dtype)
        pr_ref[:, lo:hi] = (pr * sq).astype(pr_ref.dtype)
        summ_ref[0, 0, :, lo:hi] = hf_end
        summ_ref[0, 1, :, lo:hi] = pf_end
        summ_ref[0, 2, :, lo:hi] = hr_end
        summ_ref[0, 3, :, lo:hi] = pr_end


def _input_projection(x3, g, mod, w_in_bf, w_conv_a, w_conv_b, b_conv_b, wbd, ba, bx, lam):
    n_seg, seg, d = x3.shape
    w = w_conv_a.shape[1]
    ti = GRID_W
    tb = ti * n_seg
    nb = seg // ti
    const = lambda shape: pl.BlockSpec(shape, lambda k: (0,) * len(shape))
    rows = lambda: pl.BlockSpec((tb, w), lambda k: (k, 0))
    f32 = jnp.float32
    return pl.pallas_call(
        _proj_kernel,
        grid=(nb,),
        in_specs=[pl.BlockSpec((n_seg, ti, d), lambda k: (0, k, 0)), const((1, d)), const(mod.shape),
                  pl.BlockSpec(w_in_bf.shape, lambda k: (0, 0), pipeline_mode=pl.Buffered(1)),
                  const(w_conv_a.shape), const(w_conv_b.shape), const((1, w)),
                  const(wbd.shape), const(ba.shape), const(bx.shape), const(lam.shape)],
        out_specs=[rows(), rows(), rows(), rows(),
                   pl.BlockSpec((1, 4, n_seg, w), lambda k: (k, 0, 0, 0))],
        out_shape=[jax.ShapeDtypeStruct((nb * tb, w), jnp.bfloat16),
                   jax.ShapeDtypeStruct((nb * tb, w), jnp.bfloat16),
                   jax.ShapeDtypeStruct((nb * tb, w), jnp.bfloat16),
                   jax.ShapeDtypeStruct((nb * tb, w), jnp.bfloat16),
                   jax.ShapeDtypeStruct((nb, 4, n_seg, w), f32)],
        scratch_shapes=[pltpu.VMEM((d // LANES, tb, LANES), f32)],
        compiler_params=pltpu.CompilerParams(dimension_semantics=("arbitrary",),
                                             vmem_limit_bytes=VMEM_LIMIT),
        name="proj",
    )(x3, g, mod, w_in_bf, w_conv_a, w_conv_b, b_conv_b, wbd, ba, bx, lam)


def _carry_kernel(summ_ref, h0_ref, cf_ref, cr_ref):
    nb, _, n_seg, _ = summ_ref.shape
    c = h0_ref[0:1, :]
    for s in range(n_seg):
        for k in range(nb):
            cf_ref[k, s:s + 1, :] = c
            c = summ_ref[k, 0, s:s + 1, :] + summ_ref[k, 1, s:s + 1, :] * c
    c = h0_ref[1:2, :]
    for s in range(n_seg - 1, -1, -1):
        for k in range(nb - 1, -1, -1):
            cr_ref[k, s:s + 1, :] = c
            c = summ_ref[k, 2, s:s + 1, :] + summ_ref[k, 3, s:s + 1, :] * c


def _chunk_carries(summ, h0):
    nb, _, n_seg, w = summ.shape
    full = lambda shape: pl.BlockSpec(shape, lambda i: (0,) * len(shape))
    return pl.pallas_call(
        _carry_kernel,
        grid=(1,),
        in_specs=[full(summ.shape), full(h0.shape)],
        out_specs=[full((nb, n_seg, w)), full((nb, n_seg, w))],
        out_shape=[jax.ShapeDtypeStruct((nb, n_seg, w), jnp.float32)] * 2,
        compiler_params=pltpu.CompilerParams(dimension_semantics=("arbitrary",)),
        name="carry",
    )(summ, h0)


def _out_kernel(x_ref, ya_ref, s_ref, pf_ref, pr_ref, cf_ref, cr_ref, wf_ref, mod_ref, g_ref, o_ref, r_s, w_ref):
    n_seg, ti, d = x_ref.shape
    wdt = ya_ref.shape[1]
    n_slab = d // LANES

    @pl.when(pl.program_id(0) == 0)
    def _():
        rows = MXU_TILE

        def cast_rows(i, carry):
            r0 = pl.multiple_of(i * rows, rows)
            w_ref[pl.ds(r0, rows), :] = wf_ref[pl.ds(r0, rows), :].astype(w_ref.dtype)
            return carry

        lax.fori_loop(0, wf_ref.shape[0] // rows, cast_rows, 0)

    cf = jnp.tile(cf_ref[0], (ti, 1))
    cr = jnp.tile(cr_ref[0], (ti, 1))
    f32 = jnp.float32
    yb = (s_ref[...].astype(f32) + pf_ref[...].astype(f32) * cf
          + pr_ref[...].astype(f32) * cr).astype(jnp.bfloat16)
    out = jnp.dot(ya_ref[...], w_ref[0:wdt, :], preferred_element_type=jnp.float32)
    out = out + jnp.dot(yb, w_ref[wdt:2 * wdt, :], preferred_element_type=jnp.float32)
    for c in range(n_slab):
        r_s[c] = out[:, c * LANES:(c + 1) * LANES]
    gate = mod_ref[0:1, 2 * d:3 * d]
    for s in range(n_seg):
        out_s = jnp.concatenate([r_s[c, pl.ds(s, ti, stride=n_seg), :] for c in range(n_slab)], axis=1)
        new = x_ref[s] + gate * out_s
        ms = jnp.mean(new * new, axis=-1, keepdims=True)
        o_ref[s] = (new * lax.rsqrt(ms + EPS)) * g_ref[...]


def _output_projection(x3, ya, s, pf, pr, cf, cr, w_out, mod, final_g):
    n_seg, seg, d = x3.shape
    w = ya.shape[1]
    ti = GRID_W
    tb = ti * n_seg
    nb = seg // ti
    const = lambda shape: pl.BlockSpec(shape, lambda k: (0,) * len(shape))
    rows = lambda: pl.BlockSpec((tb, w), lambda k: (k, 0))
    segs = lambda: pl.BlockSpec((n_seg, ti, d), lambda k: (0, k, 0))
    carry = lambda: pl.BlockSpec((1, n_seg, w), lambda k: (k, 0, 0))
    return pl.pallas_call(
        _out_kernel,
        grid=(nb,),
        in_specs=[segs(), rows(), rows(), rows(), rows(), carry(), carry(),
                  pl.BlockSpec(w_out.shape, lambda k: (0, 0), pipeline_mode=pl.Buffered(1)),
                  const(mod.shape), const((1, d))],
        out_specs=segs(),
        out_shape=jax.ShapeDtypeStruct((n_seg, seg, d), jnp.float32),
        scratch_shapes=[pltpu.VMEM((d // LANES, tb, LANES), jnp.float32),
                        pltpu.VMEM(w_out.shape, jnp.bfloat16)],
        compiler_params=pltpu.CompilerParams(dimension_semantics=("arbitrary",),
                                             vmem_limit_bytes=VMEM_LIMIT),
        name="out",
    )(x3, ya, s, pf, pr, cf, cr, w_out, mod, final_g)


def _permute_projection_columns(w_in, width):
    d = w_in.shape[0]
    n_groups = width // MXU_TILE
    wg = w_in.reshape(d, 6, n_groups, MXU_TILE)
    cols = [wg[:, 4, 0]]
    for g in range(n_groups):
        if g + 1 < n_groups:
            cols.append(wg[:, 4, g + 1])
        cols += [wg[:, 5, g], wg[:, 1, g], wg[:, 2, g], wg[:, 0, g], wg[:, 3, g]]
    return jnp.concatenate(cols, axis=1).astype(jnp.bfloat16)


def kernel(x, c, ctx, c_ctx, norm_g, w_ada, b_ada, w_in, w_conv_a, w_conv_b, b_conv_b,
           lru_wa, lru_ba, lru_wx, lru_bx, lru_lambda, w_out, final_g):
    assert w_in.shape[0] == 1 and x.shape[0] == 1, "single layer, single batch element"
    _, l, d = x.shape
    assert l % (N_SEG * GRID_W) == 0

    mod, wbd = _modulation(c, c_ctx[None, :], w_ada[0], b_ada, lru_wa[0], lru_wx[0])
    g = norm_g

    w_in_bf = _permute_projection_columns(w_in[0], w_conv_a.shape[-1])
    ba, bx, lam = lru_ba[0], lru_bx[0], lru_lambda[0]

    x3 = x[0].reshape(N_SEG, l // N_SEG, d)
    h0 = _context_state(ctx[0], g, mod, w_in[0], w_conv_b[0], b_conv_b, wbd, ba, bx, lam)
    ya, s, pf, pr, summ = _input_projection(x3, g, mod, w_in_bf, w_conv_a[0], w_conv_b[0], b_conv_b,
                                            wbd, ba, bx, lam)
    cf, cr = _chunk_carries(summ, h0)
    out = _output_projection(x3, ya, s, pf, pr, cf, cr, w_out[0], mod, final_g[None, :])
    return out.reshape(1, l, d)
```

```python
import jax
import jax.numpy as jnp
from jax import lax
from jax.experimental import pallas as pl
from jax.experimental.pallas import tpu as pltpu

EPS = 1e-6
LRU_C = 8.0
GRID_W = 64
CONV_A_LEFT = 1
CONV_B_LEFT = 2
SUBLANES = 8
LANES = 128
MXU_TILE = 256
N_SEG = SUBLANES
VMEM_LIMIT = 60 * 1024 * 1024


def _sigmoid(z):
    return 0.5 * jnp.tanh(0.5 * z) + 0.5


def _silu(z):
    return z * _sigmoid(z)


def _rms_mod(x, g_ref, mod_ref, row):
    d = x.shape[-1]
    shift = mod_ref[row:row + 1, 0:d]
    scale = g_ref[...] * (1.0 + mod_ref[row:row + 1, d:2 * d])
    ms = jnp.mean(x * x, axis=-1, keepdims=True)
    return (x * lax.rsqrt(ms + EPS)) * scale + shift


def _c_softplus_neg(lam):
    z = -lam
    return LRU_C * (jnp.maximum(z, 0.0) + jnp.log1p(jnp.exp(-jnp.abs(z))))


def _gate_preact(xg, wbd):
    return jnp.dot(xg.astype(jnp.bfloat16), wbd, preferred_element_type=jnp.float32)


def _lru_coeffs(xg, pre_r, pre_i, ba, bx, c8):
    r = _sigmoid(pre_r + ba)
    i = _sigmoid(pre_i + bx)
    u = r * c8
    a = jnp.exp(-u)
    one_minus_a2 = jnp.tanh(u) * (1.0 + a * a)
    return a, jnp.sqrt(one_minus_a2) * (i * xg)


def _mod_kernel(c_ref, cx_ref, w_ref, b_ref, wa_ref, wx_ref, o_ref, wbd_ref):
    rows, d = o_ref.shape[0], c_ref.shape[1]
    row = lax.broadcasted_iota(jnp.int32, (rows, d), 0)
    cc = jnp.where(row == 0, jnp.broadcast_to(c_ref[...], (rows, d)),
                   jnp.where(row == 1, jnp.broadcast_to(cx_ref[...], (rows, d)), 0.0))
    o_ref[...] = jnp.dot(_silu(cc), w_ref[...], preferred_element_type=jnp.float32) + b_ref[...]

    @pl.when(pl.program_id(0) == 0)
    def _():
        n_dir, n_heads, dh, _ = wa_ref.shape
        per = MXU_TILE // dh
        wbd_ref[...] = jnp.zeros(wbd_ref.shape, wbd_ref.dtype)
        for dr in range(n_dir):
            for h in range(n_heads):
                g, r0 = h // per, (h % per) * dh
                wbd_ref[dr, g, r0:r0 + dh, r0:r0 + dh] = wa_ref[dr, h].astype(wbd_ref.dtype)
                wbd_ref[dr, g, r0:r0 + dh, MXU_TILE + r0:MXU_TILE + r0 + dh] = wx_ref[dr, h].astype(wbd_ref.dtype)


def _modulation(c, c_ctx, w_ada, b_ada, wa, wx):
    d = c.shape[1]
    n = w_ada.shape[1]
    n_dir, n_heads, dh, _ = wa.shape
    tn = 1024
    wbd_shape = (n_dir, n_heads * dh // MXU_TILE, MXU_TILE, 2 * MXU_TILE)
    const = lambda shape: pl.BlockSpec(shape, lambda j: (0,) * len(shape))
    return pl.pallas_call(
        _mod_kernel,
        grid=(n // tn,),
        in_specs=[const((1, d)), const((1, d)),
                  pl.BlockSpec((d, tn), lambda j: (0, j)),
                  pl.BlockSpec((1, tn), lambda j: (0, j)),
                  const(wa.shape), const(wx.shape)],
        out_specs=[pl.BlockSpec((SUBLANES, tn), lambda j: (0, j)), const(wbd_shape)],
        out_shape=[jax.ShapeDtypeStruct((SUBLANES, n), jnp.float32),
                   jax.ShapeDtypeStruct(wbd_shape, jnp.bfloat16)],
        compiler_params=pltpu.CompilerParams(dimension_semantics=("arbitrary",),
                                             vmem_limit_bytes=VMEM_LIMIT),
        name="mod",
    )(c, c_ctx, w_ada, b_ada, wa, wx)


def _dwconv_seq(x, w, left):
    t = x.shape[0]
    pos = lax.broadcasted_iota(jnp.int32, x.shape, 0)
    out = None
    for j in range(w.shape[0]):
        off = j - left
        if off == 0:
            term = x * w[j:j + 1, :]
        else:
            shifted = pltpu.roll(x, (-off) % t, 0)
            valid = (pos + off >= 0) & (pos + off < t)
            term = jnp.where(valid, shifted, 0.0) * w[j:j + 1, :]
        out = term if out is None else out + term
    return out


def _tile_scan(a, b, reverse):
    sub = lax.broadcasted_iota(jnp.int32, a.shape, 0)
    for d in (1, 2, 4):
        if reverse:
            keep = sub < SUBLANES - d
            shift = SUBLANES - d
        else:
            keep = sub >= d
            shift = d
        a_sh = jnp.where(keep, pltpu.roll(a, shift, 0), 1.0)
        b_sh = jnp.where(keep, pltpu.roll(b, shift, 0), 0.0)
        b = a * b_sh + b
        a = a * a_sh
    return a, b


def _final_states(a_ref, b_ref):
    t = a_ref.shape[1]
    n_tiles = t // SUBLANES
    w = a_ref.shape[2]

    def body(j, carry):
        hf, hr = carry
        rf = pl.multiple_of(j * SUBLANES, SUBLANES)
        rr = pl.multiple_of((n_tiles - 1 - j) * SUBLANES, SUBLANES)
        af, bf = _tile_scan(a_ref[0, pl.ds(rf, SUBLANES), :], b_ref[0, pl.ds(rf, SUBLANES), :], False)
        ar, br = _tile_scan(a_ref[1, pl.ds(rr, SUBLANES), :], b_ref[1, pl.ds(rr, SUBLANES), :], True)
        yf = bf + af * hf
        yr = br + ar * hr
        hf = jnp.broadcast_to(yf[SUBLANES - 1:SUBLANES, :], (SUBLANES, w))
        hr = jnp.broadcast_to(yr[0:1, :], (SUBLANES, w))
        return hf, hr

    zero = jnp.zeros((SUBLANES, w), jnp.float32)
    return lax.fori_loop(0, n_tiles, body, (zero, zero), unroll=2)


def _ctx_kernel(x_ref, g_ref, mod_ref, wv_ref, wcb_ref, bcb_ref, wbd_ref, ba_ref, bx_ref, lam_ref,
                h0_ref, a_s, b_s):
    w = wv_ref.shape[1]
    hc = _rms_mod(x_ref[...], g_ref, mod_ref, 1)
    vc = jnp.dot(hc.astype(jnp.bfloat16), wv_ref[...].astype(jnp.bfloat16), preferred_element_type=jnp.float32)
    xb = _dwconv_seq(vc, wcb_ref[...], CONV_B_LEFT) + bcb_ref[...]
    c8 = _c_softplus_neg(lam_ref[...])
    for d in range(2):
        for g in range(w // MXU_TILE):
            lo, hi = g * MXU_TILE, (g + 1) * MXU_TILE
            pre = _gate_preact(xb[:, lo:hi], wbd_ref[d, g])
            a, b = _lru_coeffs(xb[:, lo:hi], pre[:, :MXU_TILE], pre[:, MXU_TILE:],
                               ba_ref[d:d + 1, lo:hi], bx_ref[d:d + 1, lo:hi], c8[d:d + 1, lo:hi])
            a_s[d, :, lo:hi] = a
            b_s[d, :, lo:hi] = b
    hf, hr = _final_states(a_s, b_s)
    h0_ref[0:1, :] = hf[0:1, :]
    h0_ref[1:2, :] = hr[0:1, :]


def _context_state(ctx, g, mod, w_in, w_conv_b, b_conv_b, wbd, ba, bx, lam):
    t, d = ctx.shape
    w = w_conv_b.shape[1]
    v_block = 4
    full = lambda shape: pl.BlockSpec(shape, lambda i: (0,) * len(shape))
    return pl.pallas_call(
        _ctx_kernel,
        grid=(1,),
        in_specs=[full((t, d)), full((1, d)), full(mod.shape),
                  pl.BlockSpec((d, w), lambda i: (0, v_block)),
                  full(w_conv_b.shape), full((1, w)), full(wbd.shape), full(ba.shape), full(bx.shape),
                  full(lam.shape)],
        out_specs=full((2, w)),
        out_shape=jax.ShapeDtypeStruct((2, w), jnp.float32),
        scratch_shapes=[pltpu.VMEM((2, t, w), jnp.float32), pltpu.VMEM((2, t, w), jnp.float32)],
        compiler_params=pltpu.CompilerParams(dimension_semantics=("arbitrary",),
                                             vmem_limit_bytes=VMEM_LIMIT),
        name="ctx",
    )(ctx, g, mod, w_in, w_conv_b, b_conv_b, wbd, ba, bx, lam)


def _shift_tokens(x, off):
    if off == 0:
        return x
    pad = jnp.zeros((abs(off) * N_SEG, x.shape[1]), x.dtype)
    if off < 0:
        return jnp.concatenate([pad, x[:off * N_SEG]], axis=0)
    return jnp.concatenate([x[off * N_SEG:], pad], axis=0)


def _dwconv_row(x, w, left):
    out = None
    for j in range(w.shape[0]):
        term = _shift_tokens(x, j - left) * w[j:j + 1, :]
        out = term if out is None else out + term
    return out


def _local_scan(a, b, reverse):
    n = a.shape[0] // N_SEG
    order = range(n - 1, -1, -1) if reverse else range(n)
    hs, ps = [None] * n, [None] * n
    h = p = None
    for t in order:
        at, bt = a[t * N_SEG:(t + 1) * N_SEG], b[t * N_SEG:(t + 1) * N_SEG]
        h = bt if h is None else at * h + bt
        p = at if p is None else at * p
        hs[t], ps[t] = h, p
    return jnp.concatenate(hs, axis=0), jnp.concatenate(ps, axis=0), h, p


def _proj_kernel(x_ref, g_ref, mod_ref, w_ref, wca_ref, wcb_ref, bcb_ref, wbd_ref, ba_ref, bx_ref, lam_ref,
                 ya_ref, s_ref, pf_ref, pr_ref, summ_ref, hi_s):
    n_seg, ti, d = x_ref.shape
    wdt = ya_ref.shape[1]
    n_slab = d // LANES

    for s in range(n_seg):
        hs = _rms_mod(x_ref[s], g_ref, mod_ref, 0)
        for c in range(n_slab):
            hi_s[c, pl.ds(s, ti, stride=n_seg), :] = hs[:, c * LANES:(c + 1) * LANES]
    hl = jnp.concatenate([hi_s[c] for c in range(n_slab)], axis=1).astype(jnp.bfloat16)

    def product(c0, n_cols):
        return jnp.dot(hl, w_ref[:, c0:c0 + n_cols], preferred_element_type=jnp.float32)

    gate_slabs = 2 * MXU_TILE // LANES
    assert n_slab >= 4 * gate_slabs

    def park(slab0, xg, wbd):
        half = gate_slabs // 2
        for part in range(2):
            pre = _gate_preact(xg, wbd[:, part * MXU_TILE:(part + 1) * MXU_TILE])
            for j in range(half):
                hi_s[slab0 + part * half + j] = pre[:, j * LANES:(j + 1) * LANES]

    def unpark(slab0, half):
        j0 = slab0 + half * (gate_slabs // 2)
        return jnp.concatenate([hi_s[j0 + j] for j in range(gate_slabs // 2)], axis=1)

    c8 = _c_softplus_neg(lam_ref[...])
    n_groups = wdt // MXU_TILE
    group_cols = len(_GROUP_ORDER) * MXU_TILE
    v_next = product((n_groups - 1) * group_cols, MXU_TILE)
    for g in range(n_groups):
        lo, hi = g * MXU_TILE, (g + 1) * MXU_TILE
        slab_f = (g % 2) * 2 * gate_slabs
        slab_r = slab_f + gate_slabs
        xb = _dwconv_row(v_next, wcb_ref[:, lo:hi], CONV_B_LEFT) + bcb_ref[:, lo:hi]
        xb16 = xb.astype(jnp.bfloat16)
        park(slab_f, xb16, wbd_ref.at[0, g])
        park(slab_r, xb16, wbd_ref.at[1, g])
        skip = 0 if g + 1 < n_groups else MXU_TILE
        wide = product(g * group_cols + skip, group_cols - skip)
        parts = [wide[:, j * MXU_TILE:(j + 1) * MXU_TILE] for j in range(wide.shape[1] // MXU_TILE)]
        if g + 1 < n_groups:
            v_next = parts.pop(0)
        q_lin, c_lin, u_lin, b_lin, g_lin = parts
        sq = _silu(q_lin)
        ya = b_lin * _dwconv_row(c_lin * u_lin, wca_ref[:, lo:hi], CONV_A_LEFT) * _silu(g_lin)
        ya_ref[:, lo:hi] = ya.astype(ya_ref.dtype)
        af, bf = _lru_coeffs(xb, unpark(slab_f, 0), unpark(slab_f, 1),
                             ba_ref[0:1, lo:hi], bx_ref[0:1, lo:hi], c8[0:1, lo:hi])
        hf, pf, hf_end, pf_end = _local_scan(af, bf, False)
        ar, br = _lru_coeffs(xb, unpark(slab_r, 0), unpark(slab_r, 1),
                             ba_ref[1:2, lo:hi], bx_ref[1:2, lo:hi], c8[1:2, lo:hi])
        hr, pr, hr_end, pr_end = _local_scan(ar, br, True)
        s_ref[:, lo:hi] = ((hf + hr) * sq).astype(s_ref.dtype)
        pf_ref[:, lo:hi] = (pf * sq).astype(pf_ref.dtype)
        pr_ref[:, lo:hi] = (pr * sq).astype(pr_ref.dtype)
        summ_ref[0, 0, :, lo:hi] = hf_end
        summ_ref[0, 1, :, lo:hi] = pf_end
        summ_ref[0, 2, :, lo:hi] = hr_end
        summ_ref[0, 3, :, lo:hi] = pr_end


def _input_projection(x3, g, mod, w_in_bf, w_conv_a, w_conv_b, b_conv_b, wbd, ba, bx, lam):
    n_seg, seg, d = x3.shape
    w = w_conv_a.shape[1]
    ti = GRID_W
    tb = ti * n_seg
    nb = seg // ti
    const = lambda shape: pl.BlockSpec(shape, lambda k: (0,) * len(shape))
    rows = lambda: pl.BlockSpec((tb, w), lambda k: (k, 0))
    f32 = jnp.float32
    return pl.pallas_call(
        _proj_kernel,
        grid=(nb,),
        in_specs=[pl.BlockSpec((n_seg, ti, d), lambda k: (0, k, 0)), const((1, d)), const(mod.shape),
                  pl.BlockSpec(w_in_bf.shape, lambda k: (0, 0), pipeline_mode=pl.Buffered(1)),
                  const(w_conv_a.shape), const(w_conv_b.shape), const((1, w)),
                  const(wbd.shape), const(ba.shape), const(bx.shape), const(lam.shape)],
        out_specs=[rows(), rows(), rows(), rows(),
                   pl.BlockSpec((1, 4, n_seg, w), lambda k: (k, 0, 0, 0))],
        out_shape=[jax.ShapeDtypeStruct((nb * tb, w), jnp.bfloat16),
                   jax.ShapeDtypeStruct((nb * tb, w), jnp.bfloat16),
                   jax.ShapeDtypeStruct((nb * tb, w), jnp.bfloat16),
                   jax.ShapeDtypeStruct((nb * tb, w), jnp.bfloat16),
                   jax.ShapeDtypeStruct((nb, 4, n_seg, w), f32)],
        scratch_shapes=[pltpu.VMEM((d // LANES, tb, LANES), f32)],
        compiler_params=pltpu.CompilerParams(dimension_semantics=("arbitrary",),
                                             vmem_limit_bytes=VMEM_LIMIT),
        name="proj",
    )(x3, g, mod, w_in_bf, w_conv_a, w_conv_b, b_conv_b, wbd, ba, bx, lam)


def _carry_kernel(summ_ref, h0_ref, cf_ref, cr_ref):
    nb, _, n_seg, _ = summ_ref.shape
    c = h0_ref[0:1, :]
    for s in range(n_seg):
        for k in range(nb):
            cf_ref[k, s:s + 1, :] = c
            c = summ_ref[k, 0, s:s + 1, :] + summ_ref[k, 1, s:s + 1, :] * c
    c = h0_ref[1:2, :]
    for s in range(n_seg - 1, -1, -1):
        for k in range(nb - 1, -1, -1):
            cr_ref[k, s:s + 1, :] = c
            c = summ_ref[k, 2, s:s + 1, :] + summ_ref[k, 3, s:s + 1, :] * c


def _chunk_carries(summ, h0):
    nb, _, n_seg, w = summ.shape
    full = lambda shape: pl.BlockSpec(shape, lambda i: (0,) * len(shape))
    return pl.pallas_call(
        _carry_kernel,
        grid=(1,),
        in_specs=[full(summ.shape), full(h0.shape)],
        out_specs=[full((nb, n_seg, w)), full((nb, n_seg, w))],
        out_shape=[jax.ShapeDtypeStruct((nb, n_seg, w), jnp.float32)] * 2,
        compiler_params=pltpu.CompilerParams(dimension_semantics=("arbitrary",)),
        name="carry",
    )(summ, h0)


def _out_kernel(x_ref, ya_ref, s_ref, pf_ref, pr_ref, cf_ref, cr_ref, wf_ref, mod_ref, g_ref, o_ref, r_s, w_ref):
    n_seg, ti, d = x_ref.shape
    wdt = ya_ref.shape[1]
    n_slab = d // LANES

    @pl.when(pl.program_id(0) == 0)
    def _():
        rows = MXU_TILE

        def cast_rows(i, carry):
            r0 = pl.multiple_of(i * rows, rows)
            w_ref[pl.ds(r0, rows), :] = wf_ref[pl.ds(r0, rows), :].astype(w_ref.dtype)
            return carry

        lax.fori_loop(0, wf_ref.shape[0] // rows, cast_rows, 0)

    cf = jnp.tile(cf_ref[0], (ti, 1))
    cr = jnp.tile(cr_ref[0], (ti, 1))
    f32 = jnp.float32
    yb = (s_ref[...].astype(f32) + pf_ref[...].astype(f32) * cf
          + pr_ref[...].astype(f32) * cr).astype(jnp.bfloat16)
    out = jnp.dot(ya_ref[...], w_ref[0:wdt, :], preferred_element_type=jnp.float32)
    out = out + jnp.dot(yb, w_ref[wdt:2 * wdt, :], preferred_element_type=jnp.float32)
    for c in range(n_slab):
        r_s[c] = out[:, c * LANES:(c + 1) * LANES]
    gate = mod_ref[0:1, 2 * d:3 * d]
    for s in range(n_seg):
        out_s = jnp.concatenate([r_s[c, pl.ds(s, ti, stride=n_seg), :] for c in range(n_slab)], axis=1)
        new = x_ref[s] + gate * out_s
        ms = jnp.mean(new * new, axis=-1, keepdims=True)
        o_ref[s] = (new * lax.rsqrt(ms + EPS)) * g_ref[...]


def _output_projection(x3, ya, s, pf, pr, cf, cr, w_out, mod, final_g):
    n_seg, seg, d = x3.shape
    w = ya.shape[1]
    ti = GRID_W
    tb = ti * n_seg
    nb = seg // ti
    const = lambda shape: pl.BlockSpec(shape, lambda k: (0,) * len(shape))
    rows = lambda: pl.BlockSpec((tb, w), lambda k: (k, 0))
    segs = lambda: pl.BlockSpec((n_seg, ti, d), lambda k: (0, k, 0))
    carry = lambda: pl.BlockSpec((1, n_seg, w), lambda k: (k, 0, 0))
    return pl.pallas_call(
        _out_kernel,
        grid=(nb,),
        in_specs=[segs(), rows(), rows(), rows(), rows(), carry(), carry(),
                  pl.BlockSpec(w_out.shape, lambda k: (0, 0), pipeline_mode=pl.Buffered(1)),
                  const(mod.shape), const((1, d))],
        out_specs=segs(),
        out_shape=jax.ShapeDtypeStruct((n_seg, seg, d), jnp.float32),
        scratch_shapes=[pltpu.VMEM((d // LANES, tb, LANES), jnp.float32),
                        pltpu.VMEM(w_out.shape, jnp.bfloat16)],
        compiler_params=pltpu.CompilerParams(dimension_semantics=("arbitrary",),
                                             vmem_limit_bytes=VMEM_LIMIT),
        name="out",
    )(x3, ya, s, pf, pr, cf, cr, w_out, mod, final_g)


_GROUP_ORDER = (4, 5, 1, 2, 0, 3)


def _regroup_kernel(*refs):
    o_ref = refs[-1]
    for j, w_ref in enumerate(refs[:-1]):
        o_ref[:, j * MXU_TILE:(j + 1) * MXU_TILE] = w_ref[...].astype(o_ref.dtype)


def _regroup_projection_columns(w_in, width):
    d = w_in.shape[0]
    n_groups = width // MXU_TILE
    n_parts = len(_GROUP_ORDER)

    def source(j):
        blk = _GROUP_ORDER[j] * n_groups
        if j == 0:
            return pl.BlockSpec((d, MXU_TILE), lambda g: (0, blk + lax.rem(g + 1, n_groups)))
        return pl.BlockSpec((d, MXU_TILE), lambda g: (0, blk + g))

    return pl.pallas_call(
        _regroup_kernel,
        grid=(n_groups,),
        in_specs=[source(j) for j in range(n_parts)],
        out_specs=pl.BlockSpec((d, n_parts * MXU_TILE), lambda g: (0, g)),
        out_shape=jax.ShapeDtypeStruct((d, n_parts * width), jnp.bfloat16),
        compiler_params=pltpu.CompilerParams(dimension_semantics=("arbitrary",),
                                             vmem_limit_bytes=VMEM_LIMIT),
        name="regroup",
    )(*([w_in] * n_parts))


def kernel(x, c, ctx, c_ctx, norm_g, w_ada, b_ada, w_in, w_conv_a, w_conv_b, b_conv_b,
           lru_wa, lru_ba, lru_wx, lru_bx, lru_lambda, w_out, final_g):
    assert w_in.shape[0] == 1 and x.shape[0] == 1, "single layer, single batch element"
    _, l, d = x.shape
    assert l % (N_SEG * GRID_W) == 0

    mod, wbd = _modulation(c, c_ctx[None, :], w_ada[0], b_ada, lru_wa[0], lru_wx[0])
    g = norm_g

    w_in_bf = _regroup_projection_columns(w_in[0], w_conv_a.shape[-1])
    ba, bx, lam = lru_ba[0], lru_bx[0], lru_lambda[0]

    x3 = x[0].reshape(N_SEG, l // N_SEG, d)
    h0 = _context_state(ctx[0], g, mod, w_in[0], w_conv_b[0], b_conv_b, wbd, ba, bx, lam)
    ya, s, pf, pr, summ = _input_projection(x3, g, mod, w_in_bf, w_conv_a[0], w_conv_b[0], b_conv_b,
                                            wbd, ba, bx, lam)
    cf, cr = _chunk_carries(summ, h0)
    out = _output_projection(x3, ya, s, pf, pr, cf, cr, w_out[0], mod, final_g[None, :])
    return out.reshape(1, l, d)
```
